```python
import math
import jax, jax.numpy as jnp
from jax import lax
import numpy as np

D_MODEL = 4096
BATCH = 2
SEQ = 8192
DEPTH = 2

GRID_W = 64
CTX_LEN = 256
N_MOD = 9
N_BRANCH = 4
BRANCH_W = D_MODEL // 4
S5_GROUP = 16
S5_GROUPS = BRANCH_W // S5_GROUP
S5_STATE = 64
CONV_W = 3
POOL_WINDOWS = (2, 4, 8, 16)
POOL_GROUP_W = BRANCH_W // len(POOL_WINDOWS)
MLSTM_HEADS = 4
MLSTM_DV = BRANCH_W // MLSTM_HEADS
MLSTM_DK = MLSTM_DV // 2
MLSTM_CHUNK = 64
D_FF = 3 * D_MODEL // 2
EPS = 1e-6
F32 = jnp.float32
IN_SPLITS = (
    BRANCH_W,
    BRANCH_W, BRANCH_W, BRANCH_W,
    BRANCH_W,
    MLSTM_HEADS * MLSTM_DK, MLSTM_HEADS * MLSTM_DK,
    BRANCH_W, BRANCH_W,
    4 * MLSTM_HEADS,
    N_BRANCH * D_MODEL,
)
D_IN = sum(IN_SPLITS)

kernel_name = 'hybrid_s5_conv_pool_mlstm_prefix_dit_block'


def _split_cols(z):
    offsets = []
    acc = 0
    for s in IN_SPLITS[:-1]:
        acc += s
        offsets.append(acc)
    return jnp.split(z, offsets, axis=-1)


def _rms_norm(x):
    xf = x.astype(F32)
    return (xf * lax.rsqrt(jnp.mean(xf * xf, axis=-1, keepdims=True) + EPS)).astype(x.dtype)


def _modulate(xn, shift, scale):
    return xn * (1.0 + scale) + shift


def _swiglu(u, w_i, w_o):
    a, g = jnp.split(u @ w_i, 2, axis=-1)
    return (jax.nn.silu(g) * a) @ w_o


def _ffn_sublayer(h, shift, scale, gate, w_i, w_o):
    return h + 0.5 * gate * _swiglu(_modulate(_rms_norm(h), shift, scale), w_i, w_o)


def _raster_to_column(u):
    b, t, ch = u.shape
    rows = t // GRID_W
    return u.reshape(b, rows, GRID_W, ch).transpose(0, 2, 1, 3).reshape(b, t, ch)


def _column_to_raster(u):
    b, t, ch = u.shape
    rows = t // GRID_W
    return u.reshape(b, GRID_W, rows, ch).transpose(0, 2, 1, 3).reshape(b, t, ch)


def _s5_discretise(a_re, a_im, log_dt, b_re, b_im):
    a_re, a_im = a_re.astype(F32), a_im.astype(F32)
    dt = jnp.exp(log_dt.astype(F32))[:, None]
    mag = jnp.exp(dt * a_re)
    lam_re, lam_im = mag * jnp.cos(dt * a_im), mag * jnp.sin(dt * a_im)
    den = a_re * a_re + a_im * a_im
    z_re = ((lam_re - 1.0) * a_re + lam_im * a_im) / den
    z_im = (lam_im * a_re - (lam_re - 1.0) * a_im) / den
    b_re, b_im = b_re.astype(F32), b_im.astype(F32)
    bb_re = z_re[..., None] * b_re - z_im[..., None] * b_im
    bb_im = z_re[..., None] * b_im + z_im[..., None] * b_re
    return lam_re, lam_im, bb_re, bb_im


def _cdiag_combine(e1, e2):
    a1r, a1i, b1r, b1i = e1
    a2r, a2i, b2r, b2i = e2
    return (a2r * a1r - a2i * a1i, a2r * a1i + a2i * a1r,
            a2r * b1r - a2i * b1i + b2r, a2r * b1i + a2i * b1r + b2i)


def _s5_time_major(u):
    b, t, _ = u.shape
    return jnp.moveaxis(u.astype(F32).reshape(b, t, S5_GROUPS, S5_GROUP), 1, 0)


def _s5_states(u_tm, disc, x0, reverse):
    lam_re, lam_im, bb_re, bb_im = disc
    bu_re = jnp.einsum('tbgp,gnp->tbgn', u_tm, bb_re)
    bu_im = jnp.einsum('tbgp,gnp->tbgn', u_tm, bb_im)
    if reverse:
        bu_re, bu_im = bu_re[::-1], bu_im[::-1]
    x0_re, x0_im = x0
    bu_re = bu_re.at[0].add(lam_re * x0_re - lam_im * x0_im)
    bu_im = bu_im.at[0].add(lam_re * x0_im + lam_im * x0_re)
    t = u_tm.shape[0]
    a_re = jnp.broadcast_to(lam_re, (t, 1) + lam_re.shape)
    a_im = jnp.broadcast_to(lam_im, (t, 1) + lam_im.shape)
    _, _, x_re, x_im = lax.associative_scan(_cdiag_combine, (a_re, a_im, bu_re, bu_im), axis=0)
    final = (x_re[-1], x_im[-1])
    if reverse:
        x_re, x_im = x_re[::-1], x_im[::-1]
    return x_re, x_im, final


def _s5_readout(x_re, x_im, c_re, c_im):
    return jnp.einsum('tbgn,gpn->tbgp', x_re, c_re) - jnp.einsum('tbgn,gpn->tbgp', x_im, c_im)


def _s5_glu(y_tm, w_glu, b_glu):
    t, b = y_tm.shape[:2]
    y = jnp.moveaxis(y_tm, 0, 1).reshape(b, t, BRANCH_W)
    g = jax.nn.gelu(y)
    return g * jax.nn.sigmoid(g @ w_glu.astype(F32) + b_glu.astype(F32))


def s5_branch(u_lat, u_ctx, with_ctx_out, a_re, a_im, log_dt, b_re, b_im, c_re, c_im, d_skip, w_glu, b_glu):
    ul = _s5_time_major(u_lat)
    uc = _s5_time_major(u_ctx)
    zero = jnp.zeros((u_ctx.shape[0], S5_GROUPS, S5_STATE), F32)
    d_mat = d_skip.astype(F32).reshape(S5_GROUPS, S5_GROUP)
    y_lat = d_mat * ul
    y_ctx = d_mat * uc if with_ctx_out else None
    for di, rev in enumerate((False, True)):
        disc = _s5_discretise(a_re[di], a_im[di], log_dt[di], b_re[di], b_im[di])
        cr, ci = c_re[di].astype(F32), c_im[di].astype(F32)
        xc_re, xc_im, fin = _s5_states(uc, disc, (zero, zero), rev)
        xl_re, xl_im, _ = _s5_states(ul, disc, fin, rev)
        y_lat = y_lat + _s5_readout(xl_re, xl_im, cr, ci)
        if with_ctx_out:
            y_ctx = y_ctx + _s5_readout(xc_re, xc_im, cr, ci)
    out_lat = _s5_glu(y_lat, w_glu, b_glu).astype(u_lat.dtype)
    out_ctx = _s5_glu(y_ctx, w_glu, b_glu).astype(u_ctx.dtype) if with_ctx_out else None
    return out_lat, out_ctx


def conv_branch(h, b_gate, c_gate, conv_w):
    xg = c_gate * h
    t = xg.shape[1]
    pad = CONV_W // 2
    xp = jnp.pad(xg, ((0, 0), (pad, pad), (0, 0)))
    y = conv_w[0] * xp[:, 0:t]
    for j in range(1, CONV_W):
        y = y + conv_w[j] * xp[:, j:j + t]
    return b_gate * y


def pool_branch(u, pool_w, pool_scale):
    b, t, _ = u.shape
    uf = u.astype(F32).reshape(b, t, len(POOL_WINDOWS), POOL_GROUP_W)
    cs = jnp.concatenate([jnp.zeros((b, 1) + uf.shape[2:], F32), jnp.cumsum(uf, axis=1)], axis=1)
    pos = jnp.arange(t)
    outs = []
    for gi, win in enumerate(POOL_WINDOWS):
        lo = jnp.clip(pos - win // 2, 0, t)
        hi = jnp.clip(pos + win - win // 2, 0, t)
        csg = cs[:, :, gi]
        mean = (csg[:, hi] - csg[:, lo]) / (hi - lo).astype(F32)[None, :, None]
        outs.append(mean - uf[:, :, gi])
    p = jnp.stack(outs, axis=2)
    y = jnp.einsum('btgc,gcd->btgd', p, pool_w.astype(F32)).reshape(b, t, BRANCH_W)
    return (y * pool_scale.astype(F32)).astype(u.dtype)


def _mlstm_chunk_scan(q, k, v, log_i, log_f, state):
    b, nh, t, _ = q.shape
    nc = t // MLSTM_CHUNK

    def to_chunks(a):
        return jnp.moveaxis(a.reshape(a.shape[:2] + (nc, MLSTM_CHUNK) + a.shape[3:]), 2, 0)

    causal = jnp.tril(jnp.ones((MLSTM_CHUNK, MLSTM_CHUNK), dtype=bool))

    def step(carry, blk):
        c_mat, n_vec, m = carry
        qc, kc, vc, lic, lfc = blk
        bcum = jnp.cumsum(lfc, axis=-1)
        dmat = jnp.where(causal, bcum[..., :, None] - bcum[..., None, :] + lic[..., None, :], -jnp.inf)
        m_inter = bcum + m[..., None]
        m_t = jnp.maximum(jnp.max(dmat, axis=-1), m_inter)
        w = jnp.einsum('bhtd,bhsd->bhts', qc, kc) * jnp.exp(dmat - m_t[..., None])
        decay = jnp.exp(m_inter - m_t)
        num = jnp.einsum('bhts,bhsv->bhtv', w, vc) + decay[..., None] * jnp.einsum('bhtd,bhdv->bhtv', qc, c_mat)
        den = jnp.sum(w, axis=-1) + decay * jnp.einsum('bhtd,bhd->bht', qc, n_vec)
        h = num / jnp.maximum(jnp.abs(den), jnp.exp(-m_t))[..., None]
        b_last = bcum[..., -1]
        g = b_last[..., None] - bcum + lic
        m_new = jnp.maximum(b_last + m, jnp.max(g, axis=-1))
        carry_decay = jnp.exp(b_last + m - m_new)
        wk = kc * jnp.exp(g - m_new[..., None])[..., None]
        c_new = carry_decay[..., None, None] * c_mat + jnp.einsum('bhsd,bhsv->bhdv', wk, vc)
        n_new = carry_decay[..., None] * n_vec + jnp.sum(wk, axis=-2)
        return (c_new, n_new, m_new), h

    xs = (to_chunks(q), to_chunks(k), to_chunks(v), to_chunks(log_i), to_chunks(log_f))
    final, hs = lax.scan(step, state, xs)
    h = jnp.moveaxis(hs, 0, 2).reshape(b, nh, t, hs.shape[-1])
    return h, final


def _mlstm_direction(q, k, v, log_i, log_f, state, reverse):
    if reverse:
        q, k, v = q[:, :, ::-1], k[:, :, ::-1], v[:, :, ::-1]
        log_i, log_f = log_i[..., ::-1], log_f[..., ::-1]
    h, final = _mlstm_chunk_scan(q, k, v, log_i, log_f, state)
    if reverse:
        h = h[:, :, ::-1]
    return h, final


def _mlstm_heads(z, gate_bias):
    q, k, v, _, g = z
    b, t, _ = q.shape

    def split_heads(a, dh):
        return a.astype(F32).reshape(b, t, MLSTM_HEADS, dh).transpose(0, 2, 1, 3)

    qh = split_heads(q, MLSTM_DK) * (MLSTM_DK ** -0.5)
    kh = split_heads(k, MLSTM_DK)
    vh = split_heads(v, MLSTM_DV)
    pre = (g.astype(F32) + gate_bias.astype(F32).reshape(-1)).reshape(b, t, 4, MLSTM_HEADS).transpose(2, 0, 3, 1)
    log_i = (pre[0], pre[1])
    log_f = (jax.nn.log_sigmoid(pre[2]), jax.nn.log_sigmoid(pre[3]))
    return qh, kh, vh, log_i, log_f


def _mlstm_readout(h, o, norm_gain):
    b, nh, t, dv = h.shape
    hn = h * lax.rsqrt(jnp.mean(h * h, axis=-1, keepdims=True) + EPS)
    hn = hn.transpose(0, 2, 1, 3).reshape(b, t, nh * dv) * norm_gain.astype(F32)
    return (hn * jax.nn.sigmoid(o.astype(F32))).astype(o.dtype)


def mlstm_branch(z_lat, z_ctx, with_ctx_out, gate_bias, norm_gain):
    ql, kl, vl, il, fl = _mlstm_heads(z_lat, gate_bias)
    qc, kc, vc, ic, fc = _mlstm_heads(z_ctx, gate_bias)
    b = qc.shape[0]
    zero = (jnp.zeros((b, MLSTM_HEADS, MLSTM_DK, MLSTM_DV), F32),
            jnp.zeros((b, MLSTM_HEADS, MLSTM_DK), F32),
            jnp.zeros((b, MLSTM_HEADS), F32))
    h_lat, h_ctx = [], []
    for di, rev in enumerate((False, True)):
        hc, fin = _mlstm_direction(qc, kc, vc, ic[di], fc[di], zero, rev)
        hl, _ = _mlstm_direction(ql, kl, vl, il[di], fl[di], fin, rev)
        h_lat.append(hl)
        h_ctx.append(hc)
    out_lat = _mlstm_readout(h_lat[0] + h_lat[1], z_lat[3], norm_gain)
    out_ctx = _mlstm_readout(h_ctx[0] + h_ctx[1], z_ctx[3], norm_gain) if with_ctx_out else None
    return out_lat, out_ctx


def _merge_branches(branches, gate_pre, w_branch, w_out):
    gates = jax.nn.sigmoid(gate_pre.astype(F32)).astype(gate_pre.dtype)
    merged = gates[..., 0:D_MODEL] * (branches[0] @ w_branch[0])
    for i in range(1, N_BRANCH):
        merged = merged + gates[..., i * D_MODEL:(i + 1) * D_MODEL] * (branches[i] @ w_branch[i])
    return merged @ w_out


def token_mixer(u_lat, u_ctx, with_ctx_out, lp):
    zl = _split_cols(u_lat @ lp['w_in'])
    zc = _split_cols(u_ctx @ lp['w_in'])
    s5_l, s5_c = s5_branch(zl[0], zc[0], with_ctx_out, lp['s5_a_re'], lp['s5_a_im'], lp['s5_log_dt'],
                           lp['s5_b_re'], lp['s5_b_im'], lp['s5_c_re'], lp['s5_c_im'], lp['s5_d'],
                           lp['s5_w_glu'], lp['s5_b_glu'])
    ml_l, ml_c = mlstm_branch(zl[5:10], zc[5:10], with_ctx_out, lp['mlstm_gate_bias'], lp['mlstm_norm_gain'])
    conv_l = conv_branch(zl[1], zl[2], zl[3], lp['conv_w'])
    pool_l = pool_branch(zl[4], lp['pool_w'], lp['pool_scale'])
    y_lat = _merge_branches((s5_l, conv_l, pool_l, ml_l), zl[10], lp['w_branch'], lp['w_out'])
    if not with_ctx_out:
        return y_lat, None
    conv_c = conv_branch(zc[1], zc[2], zc[3], lp['conv_w'])
    pool_c = pool_branch(zc[4], lp['pool_w'], lp['pool_scale'])
    y_ctx = _merge_branches((s5_c, conv_c, pool_c, ml_c), zc[10], lp['w_branch'], lp['w_out'])
    return y_lat, y_ctx


def setup_inputs(seed: int = 0) -> dict:
    key = jax.random.key(seed)
    ks = iter(jax.random.split(key, 40))

    def nrm(shape, scale):
        return jax.random.normal(next(ks), shape, F32) * scale

    L, D, W, G, N, P, H = DEPTH, D_MODEL, BRANCH_W, S5_GROUPS, S5_STATE, S5_GROUP, MLSTM_HEADS
    a_im_base = math.pi * jnp.arange(N, dtype=F32)
    f_bias = jnp.broadcast_to(jnp.linspace(3.0, 6.0, H, dtype=F32), (2, H))
    gate_offsets = jnp.concatenate([jnp.zeros((2, H), F32), f_bias], axis=0)
    return {
        'x': nrm((BATCH, SEQ, D), 1.0),
        'c': nrm((BATCH, D), 1.0),
        'ctx': nrm((BATCH, CTX_LEN, D), 1.0),
        'c_ctx': nrm((D,), 1.0),
        'w_ada': nrm((L, D, N_MOD * D), 0.5 * D ** -0.5),
        'b_ada': nrm((L, N_MOD * D), 0.02),
        'w_ffn1_in': nrm((L, D, 2 * D_FF), D ** -0.5),
        'w_ffn1_out': nrm((L, D_FF, D), D_FF ** -0.5),
        'w_ffn2_in': nrm((L, D, 2 * D_FF), D ** -0.5),
        'w_ffn2_out': nrm((L, D_FF, D), D_FF ** -0.5),
        'w_in': nrm((L, D, D_IN), D ** -0.5),
        's5_a_re': -0.5 + nrm((L, 2, G, N), 0.01),
        's5_a_im': a_im_base + nrm((L, 2, G, N), 0.01),
        's5_log_dt': jax.random.uniform(next(ks), (L, 2, G), F32, math.log(1e-3), math.log(1e-1)),
        's5_b_re': nrm((L, 2, G, N, P), (2 * P) ** -0.5),
        's5_b_im': nrm((L, 2, G, N, P), (2 * P) ** -0.5),
        's5_c_re': nrm((L, 2, G, P, N), 2 ** -0.5),
        's5_c_im': nrm((L, 2, G, P, N), 2 ** -0.5),
        's5_d': nrm((L, W), 1.0),
        's5_w_glu': nrm((L, W, W), W ** -0.5),
        's5_b_glu': nrm((L, W), 0.02),
        'conv_w': nrm((L, CONV_W, W), CONV_W ** -0.5),
        'pool_w': nrm((L, len(POOL_WINDOWS), POOL_GROUP_W, POOL_GROUP_W), POOL_GROUP_W ** -0.5),
        'pool_scale': 0.5 + nrm((L, W), 0.05),
        'mlstm_gate_bias': gate_offsets + nrm((L, 4, H), 0.1),
        'mlstm_norm_gain': 1.0 + nrm((L, W), 0.02),
        'w_branch': nrm((L, N_BRANCH, W, D), W ** -0.5),
        'w_out': nrm((L, D, D), D ** -0.5),
        'final_gain': 1.0 + nrm((D,), 0.02),
    }


def reference(x, c, ctx, c_ctx, w_ada, b_ada, w_ffn1_in, w_ffn1_out, w_ffn2_in, w_ffn2_out, w_in,
              s5_a_re, s5_a_im, s5_log_dt, s5_b_re, s5_b_im, s5_c_re, s5_c_im, s5_d, s5_w_glu, s5_b_glu,
              conv_w, pool_w, pool_scale, mlstm_gate_bias, mlstm_norm_gain, w_branch, w_out, final_gain):
    h, hc = x, ctx
    c_act = jax.nn.silu(c)
    cc_act = jax.nn.silu(c_ctx)
    for l in range(DEPTH):
        last = l == DEPTH - 1
        mod = [m[:, None, :] for m in jnp.split(c_act @ w_ada[l] + b_ada[l], N_MOD, axis=-1)]
        modc = jnp.split(cc_act @ w_ada[l] + b_ada[l], N_MOD, axis=-1)
        h = _ffn_sublayer(h, mod[0], mod[1], mod[2], w_ffn1_in[l], w_ffn1_out[l])
        hc = _ffn_sublayer(hc, modc[0], modc[1], modc[2], w_ffn1_in[l], w_ffn1_out[l])
        u = _modulate(_rms_norm(h), mod[3], mod[4])
        uc = _modulate(_rms_norm(hc), modc[3], modc[4])
        col_major = (l % 2) == 1
        if col_major:
            u = _raster_to_column(u)
        lp = {
            'w_in': w_in[l], 's5_a_re': s5_a_re[l], 's5_a_im': s5_a_im[l], 's5_log_dt': s5_log_dt[l],
            's5_b_re': s5_b_re[l], 's5_b_im': s5_b_im[l], 's5_c_re': s5_c_re[l], 's5_c_im': s5_c_im[l],
            's5_d': s5_d[l], 's5_w_glu': s5_w_glu[l], 's5_b_glu': s5_b_glu[l], 'conv_w': conv_w[l],
            'pool_w': pool_w[l], 'pool_scale': pool_scale[l], 'mlstm_gate_bias': mlstm_gate_bias[l],
            'mlstm_norm_gain': mlstm_norm_gain[l], 'w_branch': w_branch[l], 'w_out': w_out[l],
        }
        y, yc = token_mixer(u, uc, not last, lp)
        if col_major:
            y = _column_to_raster(y)
        h = h + mod[5] * y
        h = _ffn_sublayer(h, mod[6], mod[7], mod[8], w_ffn2_in[l], w_ffn2_out[l])
        if not last:
            hc = hc + modc[5] * yc
            hc = _ffn_sublayer(hc, modc[6], modc[7], modc[8], w_ffn2_in[l], w_ffn2_out[l])
    return _rms_norm(h) * final_gain
```

```python
import functools
import math

import jax
import jax.numpy as jnp
from jax import lax
from jax.experimental import pallas as pl
from jax.experimental.pallas import tpu as pltpu

F32 = jnp.float32
BF16 = jnp.bfloat16
EPS = 1e-6
GRID_W = 64
N_MOD = 9
N_BRANCH = 4
S5_P = 16
S5_N = 64
S5_CHUNK = 16
POOL_WINDOWS = (2, 4, 8, 16)
MLSTM_HEADS = 4
MLSTM_CHUNK = 64
HALO = 16
ROW_TILE = 1024
VMEM_LIMIT = 56 * 1024 * 1024
HI = lax.Precision.HIGHEST


def _params(n_grid_dims):
    return pltpu.CompilerParams(dimension_semantics=("arbitrary",) * n_grid_dims,
                                vmem_limit_bytes=VMEM_LIMIT)


def _gelu_tanh(y):
    return 0.5 * y * (1.0 + jnp.tanh(math.sqrt(2.0 / math.pi) * (y + 0.044715 * (y * y * y))))


def _log_sigmoid(x):
    return jnp.minimum(x, 0.0) - jnp.log1p(jnp.exp(-jnp.abs(x)))


def _ada_kernel(c_ref, w_ref, b_ref, o_ref):
    c = c_ref[...]
    a = (c * jax.nn.sigmoid(c)).astype(BF16)
    o_ref[0] = jnp.dot(a, w_ref[0].astype(BF16), preferred_element_type=F32) + b_ref[0]


def _ada_mod(c_all, w_ada, b_ada):
    n_layers, d, nd = w_ada.shape
    tn = 1024
    return pl.pallas_call(
        _ada_kernel,
        grid=(n_layers, nd // tn),
        in_specs=[pl.BlockSpec((8, d), lambda l, j: (0, 0)),
                  pl.BlockSpec((1, d, tn), lambda l, j: (l, 0, j)),
                  pl.BlockSpec((1, 1, tn), lambda l, j: (l, 0, j))],
        out_specs=pl.BlockSpec((1, 8, tn), lambda l, j: (l, 0, j)),
        out_shape=jax.ShapeDtypeStruct((n_layers, 8, nd), F32),
        compiler_params=_params(2), name="ada_mod",
    )(c_all, w_ada, b_ada.reshape(n_layers, 1, nd))


def _prep_kernel(h_ref, shift_ref, scale_ref, o_ref):
    x = h_ref[0]
    ms = jnp.mean(x * x, axis=-1, keepdims=True)
    o_ref[0] = (x * lax.rsqrt(ms + EPS) * (1.0 + scale_ref[0]) + shift_ref[0]).astype(o_ref.dtype)


def _prep(h, shift, scale, col_major):
    b, t, d = h.shape
    per_batch = shift.shape[0] > 1
    mod_spec = pl.BlockSpec((1, 1, d), lambda bi, i: (bi if per_batch else 0, 0, 0))
    if col_major:
        rows = t // GRID_W
        h_in = h.reshape(b, rows, GRID_W * d)
        grid = (b, GRID_W)
        in_spec = pl.BlockSpec((1, rows, d), lambda bi, i: (bi, 0, i))
        out_spec = pl.BlockSpec((1, rows, d), lambda bi, i: (bi, i, 0))
    else:
        tr = min(t, 512)
        h_in = h
        grid = (b, t // tr)
        in_spec = pl.BlockSpec((1, tr, d), lambda bi, i: (bi, i, 0))
        out_spec = in_spec
    return pl.pallas_call(
        _prep_kernel, grid=grid, in_specs=[in_spec, mod_spec, mod_spec], out_specs=out_spec,
        out_shape=jax.ShapeDtypeStruct((b, t, d), BF16),
        compiler_params=_params(2), name="prep",
    )(h_in, shift, scale)


def _final_kernel(h_ref, g_ref, o_ref):
    x = h_ref[0]
    ms = jnp.mean(x * x, axis=-1, keepdims=True)
    o_ref[0] = x * lax.rsqrt(ms + EPS) * g_ref[...]


def _final_norm(h, gain):
    b, t, d = h.shape
    tr = min(t, 512)
    spec = pl.BlockSpec((1, tr, d), lambda bi, i: (bi, i, 0))
    return pl.pallas_call(
        _final_kernel, grid=(b, t // tr),
        in_specs=[spec, pl.BlockSpec((1, d), lambda bi, i: (0, 0))], out_specs=spec,
        out_shape=jax.ShapeDtypeStruct((b, t, d), F32),
        compiler_params=_params(2), name="final_norm",
    )(h, gain.reshape(1, d))


def _unperm_kernel(h_ref, y_ref, g_ref, o_ref):
    o_ref[0] = h_ref[0] + g_ref[0] * y_ref[0]


def _unperm_resid(h, y_col, gate):
    b, t, d = h.shape
    rows = t // GRID_W
    per_batch = gate.shape[0] > 1
    ras = pl.BlockSpec((1, rows, d), lambda bi, c: (bi, 0, c))
    out = pl.pallas_call(
        _unperm_kernel, grid=(b, GRID_W),
        in_specs=[ras, pl.BlockSpec((1, rows, d), lambda bi, c: (bi, c, 0)),
                  pl.BlockSpec((1, 1, d), lambda bi, c: (bi if per_batch else 0, 0, 0))],
        out_specs=ras,
        out_shape=jax.ShapeDtypeStruct((b, rows, GRID_W * d), F32),
        compiler_params=_params(2), name="unperm_resid",
    )(h.reshape(b, rows, GRID_W * d), y_col, gate)
    return out.reshape(b, t, d)


def _mm_kernel(*refs, n_w, n_extra, epilogue):
    x_ref = refs[0]
    w_refs = refs[1:1 + n_w]
    extra = refs[1 + n_w:1 + n_w + n_extra]
    o_ref = refs[1 + n_w + n_extra]
    x = x_ref[...]
    accs = [jnp.dot(x, w[...], preferred_element_type=F32) for w in w_refs]
    o_ref[...] = epilogue(accs, x, extra).astype(o_ref.dtype)


def _mm(x, tm, tn, n_cols, w_specs, w_arrays, extra_specs, extra_arrays, epilogue, out_dtype, name):
    m, k = x.shape
    return pl.pallas_call(
        functools.partial(_mm_kernel, n_w=len(w_arrays), n_extra=len(extra_arrays), epilogue=epilogue),
        grid=(m // tm, n_cols // tn),
        in_specs=[pl.BlockSpec((tm, k), lambda i, j: (i, 0))] + list(w_specs) + list(extra_specs),
        out_specs=pl.BlockSpec((tm, tn), lambda i, j: (i, j)),
        out_shape=jax.ShapeDtypeStruct((m, n_cols), out_dtype),
        compiler_params=_params(2), name=name,
    )(x, *w_arrays, *extra_arrays)


def _layer_w_spec(l, k, tn, col_block_offset=0):
    return pl.BlockSpec((None, k, tn), lambda i, j: (l, 0, col_block_offset + j))


def _gate_spec(gate, tm, rows_per_batch, tn):
    per_batch = gate.shape[0] > 1
    return pl.BlockSpec((1, 1, tn), lambda i, j: ((i * tm) // rows_per_batch if per_batch else 0, 0, j))


def _swiglu_epilogue(accs, x, extra):
    a, g = accs
    return g * jax.nn.sigmoid(g) * a


def _resid_epilogue(accs, x, extra, *, coef):
    h_ref, gate_ref = extra
    return h_ref[...] + (coef * gate_ref[0]) * accs[0]


def _plain_epilogue(accs, x, extra):
    return accs[0]


def _bias_epilogue(accs, x, extra):
    return accs[0] + extra[0][...]


def _glu_epilogue(accs, x, extra):
    return x.astype(F32) * jax.nn.sigmoid(accs[0] + extra[0][...])


def _ffn(h, shift, scale, gate, w_in, w_out, l):
    b, t, d = h.shape
    m = b * t
    d_ff = w_out.shape[1]
    tm = min(m, ROW_TILE)
    tn = 512
    xn = _prep(h, shift, scale, False).reshape(m, d)
    act = _mm(xn, tm, tn, d_ff,
              [_layer_w_spec(l, d, tn), _layer_w_spec(l, d, tn, d_ff // tn)], [w_in, w_in],
              [], [], _swiglu_epilogue, BF16, "ffn_in")
    out = _mm(act, tm, tn, d,
              [_layer_w_spec(l, d_ff, tn)], [w_out],
              [pl.BlockSpec((tm, tn), lambda i, j: (i, j)), _gate_spec(gate, tm, t, tn)],
              [h.reshape(m, d), gate],
              functools.partial(_resid_epilogue, coef=0.5), F32, "ffn_out")
    return out.reshape(b, t, d)


def _merge_kernel(u_ref, b0, b1, b2, b3, wg_ref, wb_ref, o_ref, acc_ref):
    g = pl.program_id(2)
    gate = jax.nn.sigmoid(jnp.dot(u_ref[...], wg_ref[...], preferred_element_type=F32))
    for k, br in enumerate((b0, b1, b2, b3)):
        @pl.when(g == k)
        def _(k=k, br=br):
            val = gate * jnp.dot(br[...], wb_ref[...], preferred_element_type=F32)
            if k == 0:
                acc_ref[...] = val
            else:
                acc_ref[...] += val

    @pl.when(g == N_BRANCH - 1)
    def _():
        o_ref[...] = acc_ref[...].astype(o_ref.dtype)


def _merge(u, branches, w_gates, w_branch, l):
    m, d = u.shape
    w = branches[0].shape[1]
    tm = min(m, ROW_TILE)
    tn = 512
    nj = d // tn
    br_spec = pl.BlockSpec((tm, w), lambda i, j, g: (i, 0))
    return pl.pallas_call(
        _merge_kernel, grid=(m // tm, nj, N_BRANCH),
        in_specs=[pl.BlockSpec((tm, d), lambda i, j, g: (i, 0)), br_spec, br_spec, br_spec, br_spec,
                  pl.BlockSpec((None, d, tn), lambda i, j, g: (l, 0, g * nj + j)),
                  pl.BlockSpec((None, None, w, tn), lambda i, j, g: (l, g, 0, j))],
        out_specs=pl.BlockSpec((tm, tn), lambda i, j, g: (i, j)),
        out_shape=jax.ShapeDtypeStruct((m, d), BF16),
        scratch_shapes=[pltpu.VMEM((tm, tn), F32)],
        compiler_params=_params(3), name="merge",
    )(u, *branches, w_gates, w_branch)


def _s5_tables(a_re, a_im, log_dt, b_re, b_im, c_re, c_im, d_skip):
    a_re, a_im = a_re.astype(F32), a_im.astype(F32)
    n_groups = a_re.shape[1]
    lc, p, n = S5_CHUNK, S5_P, S5_N
    dt = jnp.exp(log_dt.astype(F32))[:, :, None]
    ks = jnp.arange(lc + 1, dtype=F32)[:, None, None, None]
    mag = jnp.exp(ks * (dt * a_re))
    pw_re, pw_im = mag * jnp.cos(ks * (dt * a_im)), mag * jnp.sin(ks * (dt * a_im))
    lam_re, lam_im = pw_re[1], pw_im[1]
    den = a_re * a_re + a_im * a_im
    z_re = ((lam_re - 1.0) * a_re + lam_im * a_im) / den
    z_im = (lam_im * a_re - (lam_re - 1.0) * a_im) / den
    b_re, b_im = b_re.astype(F32), b_im.astype(F32)
    bb_re = z_re[..., None] * b_re - z_im[..., None] * b_im
    bb_im = z_re[..., None] * b_im + z_im[..., None] * b_re
    c_re, c_im = c_re.astype(F32), c_im.astype(F32)
    cl_re = c_re[None] * pw_re[:, :, :, None, :] - c_im[None] * pw_im[:, :, :, None, :]
    cl_im = c_re[None] * pw_im[:, :, :, None, :] + c_im[None] * pw_re[:, :, :, None, :]
    kern = (jnp.einsum('tdgpn,dgnq->tdgpq', cl_re[:lc], bb_re, precision=HI)
            - jnp.einsum('tdgpn,dgnq->tdgpq', cl_im[:lc], bb_im, precision=HI))
    r_idx = jnp.arange(lc)[:, None]
    s_idx = jnp.arange(lc)[None, :]
    fwd = jnp.where((s_idx >= r_idx)[:, :, None, None, None], kern[jnp.clip(s_idx - r_idx, 0, lc - 1), 0], 0.0)
    bwd = jnp.where((r_idx >= s_idx)[:, :, None, None, None], kern[jnp.clip(r_idx - s_idx, 0, lc - 1), 1], 0.0)
    skip = (jnp.eye(lc, dtype=F32)[:, :, None, None, None]
            * (jnp.eye(p, dtype=F32)[None, None, None] * d_skip.astype(F32).reshape(1, 1, n_groups, p, 1)))
    toep = (fwd + bwd + skip).transpose(2, 0, 4, 1, 3).reshape(n_groups, lc * p, lc * p)
    e_idx = jnp.stack([jnp.arange(1, lc + 1), jnp.arange(lc, 0, -1)])
    dsel = jnp.arange(2)[:, None]
    ec_re = cl_re[e_idx, dsel]
    ec_im = -cl_im[e_idx, dsel]
    ec = jnp.stack([ec_re, ec_im]).transpose(0, 1, 3, 5, 2, 4)
    ec = ec.reshape(2, 2, n_groups // 2, 2, n, lc * p)
    pair = jnp.eye(2, dtype=F32)
    ec = ec.transpose(2, 0, 1, 3, 4, 5)[:, :, :, :, :, None, :] * pair[None, None, None, :, None, :, None]
    ec = ec.reshape(n_groups // 2, 4, 2 * n, 2 * lc * p)
    f_idx = jnp.stack([jnp.arange(lc - 1, -1, -1), jnp.arange(lc)])
    fp_re, fp_im = pw_re[f_idx, dsel], pw_im[f_idx, dsel]
    bc_re = fp_re[..., None] * bb_re[:, None] - fp_im[..., None] * bb_im[:, None]
    bc_im = fp_re[..., None] * bb_im[:, None] + fp_im[..., None] * bb_re[:, None]
    bc = jnp.stack([bc_re, bc_im]).transpose(3, 2, 5, 0, 1, 4)
    bc = bc.reshape(n_groups, lc * p, 2, 2, 1, n)
    which = jax.nn.one_hot(jnp.arange(n_groups) % 2, 2, dtype=F32)
    bc = (bc * which[:, None, None, None, :, None]).reshape(n_groups, lc * p, 8 * n)
    lam_c_re = pw_re[lc].reshape(2, 1, n_groups * n)
    lam_c_im = pw_im[lc].reshape(2, 1, n_groups * n)
    return toep.astype(BF16), ec.astype(BF16), bc.astype(BF16), lam_c_re, lam_c_im


def _s5_in_kernel(u_ref, w_ref, re_ref, im_ref):
    acc = (jnp.dot(u_ref[0], w_ref[0], preferred_element_type=F32)
           + jnp.dot(u_ref[1], w_ref[1], preferred_element_type=F32))
    w = re_ref.shape[2]
    re_ref[0] = acc[:, 0:w]
    re_ref[1] = acc[:, w:2 * w]
    im_ref[0] = acc[:, 2 * w:3 * w]
    im_ref[1] = acc[:, 3 * w:4 * w]


def _s5_scan_kernel(re_ref, im_ref, lr_ref, li_ref, ore_ref, oim_ref, *, n_batch, n_total, n_ctx):
    d = pl.program_id(0)
    lr = lr_ref[0]
    li = li_ref[0]

    def body(j, carry):
        idx = jnp.where(d == 0, j, jnp.where(j < n_ctx, n_ctx - 1 - j, n_ctx + n_total - 1 - j))
        new = []
        for bi in range(n_batch):
            xr, xi = carry[bi]
            row = bi * n_total + idx
            ore_ref[0, pl.ds(row, 1), :] = xr
            oim_ref[0, pl.ds(row, 1), :] = xi
            ur = re_ref[0, pl.ds(row, 1), :]
            ui = im_ref[0, pl.ds(row, 1), :]
            new.append((lr * xr - li * xi + ur, lr * xi + li * xr + ui))
        return tuple(new)

    zero = jnp.zeros(lr.shape, F32)
    lax.fori_loop(0, n_total, body, tuple((zero, zero) for _ in range(n_batch)))


def _s5_out_kernel(u_ref, xr_ref, xi_ref, t_ref, e_ref, o_ref):
    y01 = (jnp.dot(xr_ref[0].astype(BF16), e_ref[0, 0], preferred_element_type=F32)
           + jnp.dot(xr_ref[1].astype(BF16), e_ref[0, 1], preferred_element_type=F32)
           + jnp.dot(xi_ref[0].astype(BF16), e_ref[0, 2], preferred_element_type=F32)
           + jnp.dot(xi_ref[1].astype(BF16), e_ref[0, 3], preferred_element_type=F32))
    w = o_ref.shape[2]
    for k in range(2):
        y = y01[:, k * w:(k + 1) * w] + jnp.dot(u_ref[k], t_ref[k], preferred_element_type=F32)
        o_ref[k] = _gelu_tanh(y).astype(o_ref.dtype)


def _s5_branch(u_lat, u_ctx, tables, w_glu, b_glu, l, want_ctx):
    toep, ec, bc, lam_re, lam_im = tables
    b, t, w = u_lat.shape
    tc = u_ctx.shape[1]
    lc, p, n = S5_CHUNK, S5_P, S5_N
    g = w // p
    cw = lc * p
    n_total, n_ctx = (tc + t) // lc, tc // lc
    r = b * n_total
    u = jnp.concatenate([u_ctx, u_lat], axis=1)
    uc = u.reshape(b, n_total, lc, g, p).transpose(3, 0, 1, 2, 4).reshape(g, r, cw)
    gp = g // 2
    pair_u = pl.BlockSpec((2, r, cw), lambda i: (i, 0, 0))
    pair_x = pl.BlockSpec((2, r, 2 * n), lambda i: (0, 0, i))
    x_shape = jax.ShapeDtypeStruct((2, r, g * n), F32)
    xin_re, xin_im = pl.pallas_call(
        _s5_in_kernel, grid=(gp,),
        in_specs=[pair_u, pl.BlockSpec((2, cw, 8 * n), lambda i: (i, 0, 0))],
        out_specs=[pair_x, pair_x], out_shape=[x_shape, x_shape],
        compiler_params=_params(1), name="s5_in",
    )(uc, bc)
    lb = 512
    blk = pl.BlockSpec((1, r, lb), lambda d, j: (d, 0, j))
    lam_spec = pl.BlockSpec((1, 1, lb), lambda d, j: (d, 0, j))
    xp_re, xp_im = pl.pallas_call(
        functools.partial(_s5_scan_kernel, n_batch=b, n_total=n_total, n_ctx=n_ctx),
        grid=(2, (g * n) // lb),
        in_specs=[blk, blk, lam_spec, lam_spec], out_specs=[blk, blk], out_shape=[x_shape, x_shape],
        compiler_params=_params(2), name="s5_scan",
    )(xin_re, xin_im, lam_re, lam_im)
    yg = pl.pallas_call(
        _s5_out_kernel, grid=(gp,),
        in_specs=[pair_u, pair_x, pair_x,
                  pl.BlockSpec((2, cw, cw), lambda i: (i, 0, 0)),
                  pl.BlockSpec((1, 4, 2 * n, 2 * cw), lambda i: (i, 0, 0, 0))],
        out_specs=pair_u, out_shape=jax.ShapeDtypeStruct((g, r, cw), BF16),
        compiler_params=_params(1), name="s5_out",
    )(uc, xp_re, xp_im, toep, ec)
    y = yg.reshape(g, b, n_total, lc, p).transpose(1, 2, 3, 0, 4).reshape(b, tc + t, w)

    def glu(gx):
        m = gx.shape[0]
        tm = min(m, ROW_TILE)
        return _mm(gx, tm, w, w, [_layer_w_spec(l, w, w)], [w_glu],
                   [pl.BlockSpec((None, 1, w), lambda i, j: (l, 0, 0))], [b_glu],
                   _glu_epilogue, BF16, "s5_glu")

    out_lat = glu(y[:, tc:].reshape(b * t, w))
    out_ctx = glu(y[:, :tc].reshape(b * tc, w)) if want_ctx else None
    return out_lat, out_ctx


def _conv_kernel(hm, hp, hn, cm, cp, cn, bm, w_ref, o_ref, *, n_tiles):
    i = pl.program_id(1)
    xg = hm[0].astype(F32) * cm[0].astype(F32)
    tm = xg.shape[0]
    x_before = jnp.where(i > 0, hp[0, HALO - 1:HALO].astype(F32) * cp[0, HALO - 1:HALO].astype(F32), 0.0)
    x_after = jnp.where(i < n_tiles - 1, hn[0, 0:1].astype(F32) * cn[0, 0:1].astype(F32), 0.0)
    row = lax.broadcasted_iota(jnp.int32, xg.shape, 0)
    prev = jnp.where(row == 0, x_before, pltpu.roll(xg, 1, axis=0))
    nxt = jnp.where(row == tm - 1, x_after, pltpu.roll(xg, tm - 1, axis=0))
    w = w_ref[...]
    y = w[0:1] * prev + w[1:2] * xg + w[2:3] * nxt
    o_ref[0] = (bm[0].astype(F32) * y).astype(o_ref.dtype)


def _halo_specs(tm, width, col_block, n_halo_blocks):
    hb = tm // HALO
    main = pl.BlockSpec((1, tm, width), lambda b, i, c: (b, i, col_block + c))
    before = pl.BlockSpec((1, HALO, width), lambda b, i, c: (b, jnp.maximum(i * hb - 1, 0), col_block + c))
    after = pl.BlockSpec((1, HALO, width),
                         lambda b, i, c: (b, jnp.minimum((i + 1) * hb, n_halo_blocks - 1), col_block + c))
    return main, before, after


def _conv_branch(z, conv_w, l, col0):
    b, t, _ = z.shape
    w = conv_w.shape[2]
    tm = min(t, ROW_TILE)
    tc = 512
    n_tiles = t // tm
    cb = col0 // tc
    h_specs = _halo_specs(tm, tc, cb, t // HALO)
    bg_spec = _halo_specs(tm, tc, cb + w // tc, t // HALO)[0]
    c_specs = _halo_specs(tm, tc, cb + 2 * (w // tc), t // HALO)
    out = pl.pallas_call(
        functools.partial(_conv_kernel, n_tiles=n_tiles), grid=(b, n_tiles, w // tc),
        in_specs=[*h_specs, *c_specs, bg_spec, pl.BlockSpec((None, 3, tc), lambda bi, i, c: (l, 0, c))],
        out_specs=pl.BlockSpec((1, tm, tc), lambda bi, i, c: (bi, i, c)),
        out_shape=jax.ShapeDtypeStruct((b, t, w), BF16),
        compiler_params=_params(3), name="conv",
    )(z, z, z, z, z, z, z, conv_w)
    return out.reshape(b * t, w)


def _pool_kernel(um, up, un, w_ref, s_ref, o_ref, *, n_tiles, seq_len):
    i = pl.program_id(1)
    gi = pl.program_id(2)
    xm = um[0]
    tm = xm.shape[0]
    win = jnp.left_shift(POOL_WINDOWS[0], gi)
    half = win // 2
    x_before = jnp.where(i > 0, up[0], jnp.zeros_like(up[0]))
    x_after = jnp.where(i < n_tiles - 1, un[0], jnp.zeros_like(un[0]))
    ext = jnp.concatenate([x_before, xm, x_after], axis=0)
    s = lax.broadcasted_iota(jnp.int32, (tm, tm + 2 * HALO), 0)
    r = lax.broadcasted_iota(jnp.int32, (tm, tm + 2 * HALO), 1) - HALO
    off = r - s
    band = jnp.where(off >= -half, jnp.where(off < win - half, 1.0, 0.0), 0.0).astype(BF16)
    wsum = jnp.dot(band, ext, preferred_element_type=F32)
    t = i * tm + lax.broadcasted_iota(jnp.int32, (tm, 1), 0)
    cnt = (jnp.minimum(t + win - half, seq_len) - jnp.maximum(t - half, 0)).astype(F32)
    p = wsum / cnt - xm.astype(F32)
    y = jnp.dot(p.astype(BF16), w_ref[...], preferred_element_type=F32) * s_ref[...]
    o_ref[0] = y.astype(o_ref.dtype)


def _pool_branch(z, pool_w, pool_scale, l, col0):
    b, t, _ = z.shape
    n_groups, gw = pool_w.shape[1], pool_w.shape[2]
    tm = min(t, 256)
    n_tiles = t // tm
    specs = _halo_specs(tm, gw, col0 // gw, t // HALO)
    out = pl.pallas_call(
        functools.partial(_pool_kernel, n_tiles=n_tiles, seq_len=t), grid=(b, n_tiles, n_groups),
        in_specs=[*specs, pl.BlockSpec((None, None, gw, gw), lambda bi, i, c: (l, c, 0, 0)),
                  pl.BlockSpec((None, 1, gw), lambda bi, i, c: (l, 0, c))],
        out_specs=pl.BlockSpec((1, tm, gw), lambda bi, i, c: (bi, i, c)),
        out_shape=jax.ShapeDtypeStruct((b, t, n_groups * gw), BF16),
        compiler_params=_params(3), name="pool",
    )(z, z, z, pool_w, pool_scale)
    return out.reshape(b * t, n_groups * gw)


def _mlstm_chunk(q, k, v, li_col, lf_col, li_row, lf_row, state, reverse):
    c_mat, n_row, m = state
    lc = q.shape[0]
    ti = lax.broadcasted_iota(jnp.int32, (lc, lc), 0)
    si = lax.broadcasted_iota(jnp.int32, (lc, lc), 1)
    seen = (si >= ti) if reverse else (si <= ti)
    seen_t = (ti >= si) if reverse else (ti <= si)
    bcum_col = jnp.sum(jnp.where(seen, lf_row, 0.0), axis=1, keepdims=True)
    bcum_row = jnp.sum(jnp.where(seen_t, lf_col, 0.0), axis=0, keepdims=True)
    b_last = jnp.sum(lf_row, axis=1, keepdims=True)
    dmat = bcum_col - bcum_row + li_row
    m_inter = bcum_col + m
    m_t = jnp.maximum(jnp.max(jnp.where(seen, dmat, -1e30), axis=1, keepdims=True), m_inter)
    scores = lax.dot_general(q, k, (((1,), (1,)), ((), ())), preferred_element_type=F32)
    wgt = jnp.where(seen, scores * jnp.exp(dmat - m_t), 0.0)
    decay = jnp.exp(m_inter - m_t)
    num = (jnp.dot(wgt.astype(BF16), v, preferred_element_type=F32)
           + decay * jnp.dot(q, c_mat.astype(BF16), preferred_element_type=F32))
    den = (jnp.sum(wgt, axis=1, keepdims=True)
           + decay * jnp.sum(q.astype(F32) * n_row, axis=1, keepdims=True))
    h = num / jnp.maximum(jnp.abs(den), jnp.exp(-m_t))
    g_col = b_last - bcum_col + li_col
    m_new = jnp.maximum(b_last + m, jnp.max(g_col, axis=0, keepdims=True))
    carry = jnp.exp(b_last + m - m_new)
    wk = k.astype(F32) * jnp.exp(g_col - m_new)
    c_new = carry * c_mat + jnp.dot(wk.T.astype(BF16), v, preferred_element_type=F32)
    n_new = carry * n_row + jnp.sum(wk, axis=0, keepdims=True)
    return h, (c_new, n_new, m_new)


def _mlstm_kernel(ql, kl, vl, ol, gcl, grl, qc, kc, vc, oc, gcc, grc, gain_ref, out_l, out_c, h_l, h_c):
    lc = MLSTM_CHUNK
    dk, dv = ql.shape[2], vl.shape[2]

    def run(refs, h_ref, n_chunks, state, d):
        q_ref, k_ref, v_ref, gc_ref, gr_ref = refs

        def body(j, st):
            c = (n_chunks - 1 - j) if d == 1 else j
            t0 = pl.multiple_of(c * lc, lc)
            gcol = gc_ref[0, 0, pl.ds(t0, lc), :]
            li_col = gcol[:, d:d + 1]
            lf_col = _log_sigmoid(gcol[:, 2 + d:3 + d])
            li_row = gr_ref[0, 0, d, pl.ds(c, 1), :]
            lf_row = _log_sigmoid(gr_ref[0, 0, 2 + d, pl.ds(c, 1), :])
            h, st = _mlstm_chunk(q_ref[0, pl.ds(t0, lc), :], k_ref[0, pl.ds(t0, lc), :],
                                 v_ref[0, pl.ds(t0, lc), :], li_col, lf_col, li_row, lf_row, st, d == 1)
            if d == 0:
                h_ref[pl.ds(t0, lc), :] = h
            else:
                h_ref[pl.ds(t0, lc), :] += h
            return st

        return lax.fori_loop(0, n_chunks, body, state)

    lat = (ql, kl, vl, gcl, grl)
    ctx = (qc, kc, vc, gcc, grc)
    n_lat, n_ctx = ql.shape[1] // lc, qc.shape[1] // lc
    for d in range(2):
        state = (jnp.zeros((dk, dv), F32), jnp.zeros((1, dk), F32), jnp.zeros((1, 1), F32))
        state = run(ctx, h_c, n_ctx, state, d)
        run(lat, h_l, n_lat, state, d)

    gain = gain_ref[...]

    def readout(h_ref, o_ref, out_ref):
        t = h_ref.shape[0]
        tile = min(t, 512)

        def body(i, _):
            t0 = pl.multiple_of(i * tile, tile)
            h = h_ref[pl.ds(t0, tile), :]
            hn = h * lax.rsqrt(jnp.mean(h * h, axis=-1, keepdims=True) + EPS) * gain
            og = jax.nn.sigmoid(o_ref[0, pl.ds(t0, tile), :].astype(F32))
            out_ref[0, pl.ds(t0, tile), :] = (hn * og).astype(out_ref.dtype)
            return 0

        lax.fori_loop(0, t // tile, body, 0)

    readout(h_l, ol, out_l)
    readout(h_c, oc, out_c)


def _mlstm_branch(z_lat, zg_lat, z_ctx, zg_ctx, norm_gain, l, col_q, col_k, col_v, col_o):
    b, t, _ = z_lat.shape
    tc = z_ctx.shape[1]
    nh = MLSTM_HEADS
    w = norm_gain.shape[2]
    dv = w // nh
    dk = dv // 2
    lc = MLSTM_CHUNK

    def gate_layouts(zg, tt):
        g = zg[:, :4 * nh].reshape(b, tt, 4, nh)
        col = g.transpose(0, 3, 1, 2)
        row = g.transpose(0, 3, 2, 1).reshape(b, nh, 4, tt // lc, lc)
        return col, row

    gcl, grl = gate_layouts(zg_lat, t)
    gcc, grc = gate_layouts(zg_ctx, tc)
    once = pl.Buffered(1)

    def specs(tt):
        return [pl.BlockSpec((1, tt, dk), lambda bi, h: (bi, 0, col_q // dk + h), pipeline_mode=once),
                pl.BlockSpec((1, tt, dk), lambda bi, h: (bi, 0, col_k // dk + h), pipeline_mode=once),
                pl.BlockSpec((1, tt, dv), lambda bi, h: (bi, 0, col_v // dv + h), pipeline_mode=once),
                pl.BlockSpec((1, tt, dv), lambda bi, h: (bi, 0, col_o // dv + h), pipeline_mode=once),
                pl.BlockSpec((1, 1, tt, 4), lambda bi, h: (bi, h, 0, 0), pipeline_mode=once),
                pl.BlockSpec((1, 1, 4, tt // lc, lc), lambda bi, h: (bi, h, 0, 0, 0))]

    out_l, out_c = pl.pallas_call(
        _mlstm_kernel, grid=(b, nh),
        in_specs=specs(t) + specs(tc) + [pl.BlockSpec((None, 1, dv), lambda bi, h: (l, 0, h))],
        out_specs=[pl.BlockSpec((1, t, dv), lambda bi, h: (bi, 0, h)),
                   pl.BlockSpec((1, tc, dv), lambda bi, h: (bi, 0, h))],
        out_shape=[jax.ShapeDtypeStruct((b, t, w), BF16), jax.ShapeDtypeStruct((b, tc, w), BF16)],
        scratch_shapes=[pltpu.VMEM((t, dv), F32), pltpu.VMEM((tc, dv), F32)],
        compiler_params=_params(2), name="mlstm",
    )(z_lat, z_lat, z_lat, z_lat, gcl, grl, z_ctx, z_ctx, z_ctx, z_ctx, gcc, grc, norm_gain)
    return out_l.reshape(b * t, w), out_c.reshape(b * tc, w)


def _token_mixer(u_lat, u_ctx, want_ctx, wts, l):
    b, t, d = u_lat.shape
    tc = u_ctx.shape[1]
    w = d // 4
    n_main = 8 * w
    col_conv, col_pool, col_q, col_k, col_v, col_o = w, 4 * w, 5 * w, 5 * w + w // 2, 6 * w, 7 * w

    def in_proj(u):
        m = u.shape[0] * u.shape[1]
        u2 = u.reshape(m, d)
        tm = min(m, ROW_TILE)
        z = _mm(u2, tm, 1024, n_main, [_layer_w_spec(l, d, 1024)], [wts['w_main']], [], [],
                _plain_epilogue, BF16, "in_proj")
        zg = _mm(u2, tm, 128, 128, [_layer_w_spec(l, d, 128)], [wts['w_mgate']],
                 [pl.BlockSpec((None, 1, 128), lambda i, j: (l, 0, 0))], [wts['mgate_bias']],
                 _bias_epilogue, F32, "in_proj_gates")
        return u2, z.reshape(u.shape[0], u.shape[1], n_main), zg

    u2_lat, z_lat, zg_lat = in_proj(u_lat)
    u2_ctx, z_ctx, zg_ctx = in_proj(u_ctx)
    s5_l, s5_c = _s5_branch(z_lat[:, :, :w], z_ctx[:, :, :w], wts['s5_tables'][l],
                            wts['s5_w_glu'], wts['s5_b_glu'], l, want_ctx)
    ml_l, ml_c = _mlstm_branch(z_lat, zg_lat, z_ctx, zg_ctx, wts['mlstm_norm_gain'], l,
                               col_q, col_k, col_v, col_o)
    conv_l = _conv_branch(z_lat, wts['conv_w'], l, col_conv)
    pool_l = _pool_branch(z_lat, wts['pool_w'], wts['pool_scale'], l, col_pool)
    y_lat = _merge(u2_lat, (s5_l, conv_l, pool_l, ml_l), wts['w_mix'], wts['w_branch'], l)
    if not want_ctx:
        return y_lat, None
    conv_c = _conv_branch(z_ctx, wts['conv_w'], l, col_conv)
    pool_c = _pool_branch(z_ctx, wts['pool_w'], wts['pool_scale'], l, col_pool)
    y_ctx = _merge(u2_ctx, (s5_c, conv_c, pool_c, ml_c), wts['w_mix'], wts['w_branch'], l)
    return y_lat, y_ctx


def _out_proj_resid(h, y, gate, w_out, l, col_major):
    b, t, d = h.shape
    m = b * t
    tm = min(m, ROW_TILE)
    tn = 512
    if col_major:
        yo = _mm(y, tm, tn, d, [_layer_w_spec(l, d, tn)], [w_out], [], [], _plain_epilogue, F32, "out_proj")
        return _unperm_resid(h, yo.reshape(b, t, d), gate)
    out = _mm(y, tm, tn, d, [_layer_w_spec(l, d, tn)], [w_out],
              [pl.BlockSpec((tm, tn), lambda i, j: (i, j)), _gate_spec(gate, tm, t, tn)],
              [h.reshape(m, d), gate], functools.partial(_resid_epilogue, coef=1.0), F32, "out_proj_resid")
    return out.reshape(b, t, d)


def kernel(x, c, ctx, c_ctx, w_ada, b_ada, w_ffn1_in, w_ffn1_out, w_ffn2_in, w_ffn2_out, w_in, s5_a_re, s5_a_im, s5_log_dt, s5_b_re, s5_b_im, s5_c_re, s5_c_im, s5_d, s5_w_glu, s5_b_glu, conv_w, pool_w, pool_scale, mlstm_gate_bias, mlstm_norm_gain, w_branch, w_out, final_gain):
    n_batch, _, d = x.shape
    depth = w_ada.shape[0]
    w = d // 4
    n_main = 8 * w
    n_mgate = 4 * MLSTM_HEADS
    dk = w // MLSTM_HEADS // 2

    col_scale = jnp.ones((n_main,), F32).at[5 * w:5 * w + w // 2].set(dk ** -0.5)
    wts = {
        'w_main': (w_in[:, :, :n_main] * col_scale).astype(BF16),
        'w_mgate': jnp.pad(w_in[:, :, n_main:n_main + n_mgate], ((0, 0), (0, 0), (0, 128 - n_mgate))).astype(BF16),
        'mgate_bias': jnp.pad(mlstm_gate_bias.reshape(depth, 1, n_mgate).astype(F32),
                              ((0, 0), (0, 0), (0, 128 - n_mgate))),
        'w_mix': w_in[:, :, n_main + n_mgate:].astype(BF16),
        'w_branch': w_branch.astype(BF16),
        's5_w_glu': s5_w_glu.astype(BF16),
        's5_b_glu': s5_b_glu.astype(F32).reshape(depth, 1, w),
        'conv_w': conv_w.astype(F32),
        'pool_w': pool_w.astype(BF16),
        'pool_scale': pool_scale.astype(F32).reshape(depth, 1, w),
        'mlstm_norm_gain': mlstm_norm_gain.astype(F32).reshape(depth, 1, w),
        's5_tables': [_s5_tables(s5_a_re[l], s5_a_im[l], s5_log_dt[l], s5_b_re[l], s5_b_im[l],
                                 s5_c_re[l], s5_c_im[l], s5_d[l]) for l in range(depth)],
    }
    w1i, w1o = w_ffn1_in.astype(BF16), w_ffn1_out.astype(BF16)
    w2i, w2o = w_ffn2_in.astype(BF16), w_ffn2_out.astype(BF16)
    w_o = w_out.astype(BF16)

    c_all = jnp.zeros((8, d), F32).at[:n_batch].set(c).at[n_batch].set(c_ctx)
    mods = _ada_mod(c_all, w_ada, b_ada).reshape(depth, 8, N_MOD, d)

    h, hc = x, ctx
    for l in range(depth):
        last = l == depth - 1
        mod = [mods[l, :n_batch, k][:, None, :] for k in range(N_MOD)]
        modc = [mods[l, n_batch:n_batch + 1, k][:, None, :] for k in range(N_MOD)]
        h = _ffn(h, mod[0], mod[1], mod[2], w1i, w1o, l)
        hc = _ffn(hc, modc[0], modc[1], modc[2], w1i, w1o, l)
        col_major = (l % 2) == 1
        u = _prep(h, mod[3], mod[4], col_major)
        uc = _prep(hc, modc[3], modc[4], False)
        y, yc = _token_mixer(u, uc, not last, wts, l)
        h = _out_proj_resid(h, y, mod[5], w_o, l, col_major)
        h = _ffn(h, mod[6], mod[7], mod[8], w2i, w2o, l)
        if not last:
            hc = _out_proj_resid(hc, yc, modc[5], w_o, l, False)
            hc = _ffn(hc, modc[6], modc[7], modc[8], w2i, w2o, l)
    return _final_norm(h, final_gain)
```

```python
import functools
import math

import jax
import jax.numpy as jnp
from jax import lax
from jax.experimental import pallas as pl
from jax.experimental.pallas import tpu as pltpu

F32 = jnp.float32
BF16 = jnp.bfloat16
EPS = 1e-6
GRID_W = 64
N_MOD = 9
N_BRANCH = 4
S5_P = 16
S5_N = 64
S5_CHUNK = 16
LANES = 128
S5_LANE_GROUPS = LANES // S5_P
POOL_WINDOWS = (2, 4, 8, 16)
MLSTM_HEADS = 4
MLSTM_CHUNK = 64
HALO = 16
ROW_TILE = 1024
VMEM_LIMIT = 56 * 1024 * 1024
HI = lax.Precision.HIGHEST


def _params(n_grid_dims):
    return pltpu.CompilerParams(dimension_semantics=("arbitrary",) * n_grid_dims,
                                vmem_limit_bytes=VMEM_LIMIT)


def _gelu_tanh(y):
    return 0.5 * y * (1.0 + jnp.tanh(math.sqrt(2.0 / math.pi) * (y + 0.044715 * (y * y * y))))


def _log_sigmoid(x):
    return jnp.minimum(x, 0.0) - jnp.log1p(jnp.exp(-jnp.abs(x)))


def _ada_kernel(c_ref, w_ref, b_ref, o_ref):
    c = c_ref[...]
    a = (c * jax.nn.sigmoid(c)).astype(BF16)
    o_ref[0] = jnp.dot(a, w_ref[0].astype(BF16), preferred_element_type=F32) + b_ref[0]


def _ada_mod(c_all, w_ada, b_ada):
    n_layers, d, nd = w_ada.shape
    tn = 1024
    return pl.pallas_call(
        _ada_kernel,
        grid=(n_layers, nd // tn),
        in_specs=[pl.BlockSpec((8, d), lambda l, j: (0, 0)),
                  pl.BlockSpec((1, d, tn), lambda l, j: (l, 0, j)),
                  pl.BlockSpec((1, 1, tn), lambda l, j: (l, 0, j))],
        out_specs=pl.BlockSpec((1, 8, tn), lambda l, j: (l, 0, j)),
        out_shape=jax.ShapeDtypeStruct((n_layers, 8, nd), F32),
        compiler_params=_params(2), name="ada_mod",
    )(c_all, w_ada, b_ada.reshape(n_layers, 1, nd))


def _prep_kernel(h_ref, shift_ref, scale_ref, o_ref):
    x = h_ref[0]
    ms = jnp.mean(x * x, axis=-1, keepdims=True)
    o_ref[0] = (x * lax.rsqrt(ms + EPS) * (1.0 + scale_ref[0]) + shift_ref[0]).astype(o_ref.dtype)


def _prep(h, shift, scale):
    b, t, d = h.shape
    per_batch = shift.shape[0] > 1
    mod_spec = pl.BlockSpec((1, 1, d), lambda bi, i: (bi if per_batch else 0, 0, 0))
    tr = min(t, 512)
    spec = pl.BlockSpec((1, tr, d), lambda bi, i: (bi, i, 0))
    return pl.pallas_call(
        _prep_kernel, grid=(b, t // tr), in_specs=[spec, mod_spec, mod_spec], out_specs=spec,
        out_shape=jax.ShapeDtypeStruct((b, t, d), BF16),
        compiler_params=_params(2), name="prep",
    )(h, shift, scale)


def _final_kernel(h_ref, g_ref, o_ref):
    x = h_ref[0]
    ms = jnp.mean(x * x, axis=-1, keepdims=True)
    o_ref[0] = x * lax.rsqrt(ms + EPS) * g_ref[...]


def _final_norm(h, gain):
    b, t, d = h.shape
    tr = min(t, 512)
    spec = pl.BlockSpec((1, tr, d), lambda bi, i: (bi, i, 0))
    return pl.pallas_call(
        _final_kernel, grid=(b, t // tr),
        in_specs=[spec, pl.BlockSpec((1, d), lambda bi, i: (0, 0))], out_specs=spec,
        out_shape=jax.ShapeDtypeStruct((b, t, d), F32),
        compiler_params=_params(2), name="final_norm",
    )(h, gain.reshape(1, d))


_NT_DIMS = (((1,), (1,)), ((), ()))


def _mm_kernel(*refs, n_w, n_extra, epilogue, w_transposed):
    x_ref = refs[0]
    w_refs = refs[1:1 + n_w]
    extra = refs[1 + n_w:1 + n_w + n_extra]
    o_ref = refs[1 + n_w + n_extra]
    x = x_ref[...]
    if w_transposed:
        accs = [lax.dot_general(x, w[...], _NT_DIMS, preferred_element_type=F32) for w in w_refs]
    else:
        accs = [jnp.dot(x, w[...], preferred_element_type=F32) for w in w_refs]
    o_ref[...] = epilogue(accs, x, extra).astype(o_ref.dtype)


def _mm(x, tm, tn, n_cols, w_specs, w_arrays, extra_specs, extra_arrays, epilogue, out_dtype, name,
        w_transposed=False):
    m, k = x.shape
    return pl.pallas_call(
        functools.partial(_mm_kernel, n_w=len(w_arrays), n_extra=len(extra_arrays), epilogue=epilogue,
                          w_transposed=w_transposed),
        grid=(m // tm, n_cols // tn),
        in_specs=[pl.BlockSpec((tm, k), lambda i, j: (i, 0))] + list(w_specs) + list(extra_specs),
        out_specs=pl.BlockSpec((tm, tn), lambda i, j: (i, j)),
        out_shape=jax.ShapeDtypeStruct((m, n_cols), out_dtype),
        compiler_params=_params(2), name=name,
    )(x, *w_arrays, *extra_arrays)


def _layer_w_spec(l, k, tn, col_block_offset=0):
    return pl.BlockSpec((None, k, tn), lambda i, j: (l, 0, col_block_offset + j))


def _layer_wt_spec(l, k, tn, row_block_offset=0):
    return pl.BlockSpec((None, tn, k), lambda i, j: (l, row_block_offset + j, 0))


def _gate_spec(gate, tm, rows_per_batch, tn):
    per_batch = gate.shape[0] > 1
    return pl.BlockSpec((1, 1, tn), lambda i, j: ((i * tm) // rows_per_batch if per_batch else 0, 0, j))


def _swiglu_epilogue(accs, x, extra):
    a, g = accs
    return g * jax.nn.sigmoid(g) * a


def _resid_epilogue(accs, x, extra, *, coef):
    h_ref, gate_ref = extra
    return h_ref[...] + (coef * gate_ref[0]) * accs[0]


def _plain_epilogue(accs, x, extra):
    return accs[0]


def _bias_epilogue(accs, x, extra):
    return accs[0] + extra[0][...]


def _glu_epilogue(accs, x, extra):
    return x.astype(F32) * jax.nn.sigmoid(accs[0] + extra[0][...])


def _ffn(h, shift, scale, gate, w_in, w_out, l):
    b, t, d = h.shape
    m = b * t
    d_ff = w_out.shape[1]
    tm = min(m, ROW_TILE)
    tn = 512
    xn = _prep(h, shift, scale).reshape(m, d)
    act = _mm(xn, tm, tn, d_ff,
              [_layer_w_spec(l, d, tn), _layer_w_spec(l, d, tn, d_ff // tn)], [w_in, w_in],
              [], [], _swiglu_epilogue, BF16, "ffn_in")
    out = _mm(act, tm, tn, d,
              [_layer_w_spec(l, d_ff, tn)], [w_out],
              [pl.BlockSpec((tm, tn), lambda i, j: (i, j)), _gate_spec(gate, tm, t, tn)],
              [h.reshape(m, d), gate],
              functools.partial(_resid_epilogue, coef=0.5), F32, "ffn_out")
    return out.reshape(b, t, d)


def _merge_kernel(u_ref, b0, b1, b2, b3, wg_ref, wb_ref, o_ref, acc_ref):
    g = pl.program_id(2)
    gate = jax.nn.sigmoid(lax.dot_general(u_ref[...], wg_ref[...], _NT_DIMS, preferred_element_type=F32))
    for k, br in enumerate((b0, b1, b2, b3)):
        @pl.when(g == k)
        def _(k=k, br=br):
            val = gate * jnp.dot(br[...], wb_ref[...], preferred_element_type=F32)
            if k == 0:
                acc_ref[...] = val
            else:
                acc_ref[...] += val

    @pl.when(g == N_BRANCH - 1)
    def _():
        o_ref[...] = acc_ref[...].astype(o_ref.dtype)


def _merge(u, branches, w_gates_t, w_branch, l):
    m, d = u.shape
    w = branches[0].shape[1]
    tm = min(m, ROW_TILE)
    tn = 512
    nj = d // tn
    br_spec = pl.BlockSpec((tm, w), lambda i, j, g: (i, 0))
    return pl.pallas_call(
        _merge_kernel, grid=(m // tm, nj, N_BRANCH),
        in_specs=[pl.BlockSpec((tm, d), lambda i, j, g: (i, 0)), br_spec, br_spec, br_spec, br_spec,
                  pl.BlockSpec((None, tn, d), lambda i, j, g: (l, g * nj + j, 0)),
                  pl.BlockSpec((None, None, w, tn), lambda i, j, g: (l, g, 0, j))],
        out_specs=pl.BlockSpec((tm, tn), lambda i, j, g: (i, j)),
        out_shape=jax.ShapeDtypeStruct((m, d), BF16),
        scratch_shapes=[pltpu.VMEM((tm, tn), F32)],
        compiler_params=_params(3), name="merge",
    )(u, *branches, w_gates_t, w_branch)


def _s5_tables(a_re, a_im, log_dt, b_re, b_im, c_re, c_im, d_skip):
    a_re, a_im = a_re.astype(F32), a_im.astype(F32)
    n_groups = a_re.shape[1]
    lc, p, n = S5_CHUNK, S5_P, S5_N
    dt = jnp.exp(log_dt.astype(F32))[:, :, None]
    ks = jnp.arange(lc + 1, dtype=F32)[:, None, None, None]
    mag = jnp.exp(ks * (dt * a_re))
    pw_re, pw_im = mag * jnp.cos(ks * (dt * a_im)), mag * jnp.sin(ks * (dt * a_im))
    lam_re, lam_im = pw_re[1], pw_im[1]
    den = a_re * a_re + a_im * a_im
    z_re = ((lam_re - 1.0) * a_re + lam_im * a_im) / den
    z_im = (lam_im * a_re - (lam_re - 1.0) * a_im) / den
    b_re, b_im = b_re.astype(F32), b_im.astype(F32)
    bb_re = z_re[..., None] * b_re - z_im[..., None] * b_im
    bb_im = z_re[..., None] * b_im + z_im[..., None] * b_re
    c_re, c_im = c_re.astype(F32), c_im.astype(F32)
    cl_re = c_re[None] * pw_re[:, :, :, None, :] - c_im[None] * pw_im[:, :, :, None, :]
    cl_im = c_re[None] * pw_im[:, :, :, None, :] + c_im[None] * pw_re[:, :, :, None, :]
    kern = (jnp.einsum('tdgpn,dgnq->tdgpq', cl_re[:lc], bb_re, precision=HI)
            - jnp.einsum('tdgpn,dgnq->tdgpq', cl_im[:lc], bb_im, precision=HI))
    r_idx = jnp.arange(lc)[:, None]
    s_idx = jnp.arange(lc)[None, :]
    fwd = jnp.where((s_idx >= r_idx)[:, :, None, None, None], kern[jnp.clip(s_idx - r_idx, 0, lc - 1), 0], 0.0)
    bwd = jnp.where((r_idx >= s_idx)[:, :, None, None, None], kern[jnp.clip(r_idx - s_idx, 0, lc - 1), 1], 0.0)
    skip = (jnp.eye(lc, dtype=F32)[:, :, None, None, None]
            * (jnp.eye(p, dtype=F32)[None, None, None] * d_skip.astype(F32).reshape(1, 1, n_groups, p, 1)))
    gl = S5_LANE_GROUPS
    nb = n_groups // gl
    same = jnp.eye(gl, dtype=F32)
    toep = (fwd + bwd + skip).reshape(lc, lc, nb, gl, p, p)
    toep = toep.transpose(2, 0, 3, 5, 1, 4)[:, :, :, :, :, None, :] * same[None, None, :, None, None, :, None]
    toep = toep.reshape(nb, lc * gl * p, lc * gl * p)
    e_idx = jnp.stack([jnp.arange(1, lc + 1), jnp.arange(lc, 0, -1)])
    dsel = jnp.arange(2)[:, None]
    ec = jnp.stack([cl_re[e_idx, dsel], -cl_im[e_idx, dsel]])
    ec = ec.reshape(2, 2, lc, nb, gl, p, n).transpose(3, 0, 1, 4, 6, 2, 5)
    ec = ec[:, :, :, :, :, :, None, :] * same[None, None, None, :, None, None, :, None]
    ec = ec.reshape(nb, 4 * gl * n, lc * gl * p)
    f_idx = jnp.stack([jnp.arange(lc - 1, -1, -1), jnp.arange(lc)])
    fp_re, fp_im = pw_re[f_idx, dsel], pw_im[f_idx, dsel]
    bc_re = fp_re[..., None] * bb_re[:, None] - fp_im[..., None] * bb_im[:, None]
    bc_im = fp_re[..., None] * bb_im[:, None] + fp_im[..., None] * bb_re[:, None]
    bc = jnp.stack([bc_re, bc_im]).reshape(2, 2, lc, nb, gl, n, p)
    bc = bc.transpose(3, 2, 4, 6, 0, 1, 5)[:, :, :, :, :, :, None, :]
    bc = (bc * same[None, None, :, None, None, None, :, None]).reshape(nb, lc * gl * p, 4 * gl * n)
    lam_c_re = pw_re[lc].reshape(2, 1, n_groups * n)
    lam_c_im = pw_im[lc].reshape(2, 1, n_groups * n)
    return toep.astype(BF16), ec.astype(BF16), bc.astype(BF16), lam_c_re, lam_c_im


def _s5_proj_kernel(x_ref, w_ref, o_ref, acc_ref):
    acc = lax.dot_general(x_ref[...], w_ref[...], _NT_DIMS, preferred_element_type=F32)
    n_rows = o_ref.shape[1]
    for blk in range(o_ref.shape[0]):
        acc_ref[blk] = acc[:, blk * LANES:(blk + 1) * LANES]
        for r in range(S5_CHUNK):
            rows = acc_ref[blk, pl.ds(r, n_rows, stride=S5_CHUNK), :]
            o_ref[blk, :, r * LANES:(r + 1) * LANES] = rows.astype(o_ref.dtype)


def _s5_proj(u2, w_main_t, l, w):
    m, d = u2.shape
    tm = min(m, ROW_TILE)
    nb = w // LANES
    return pl.pallas_call(
        _s5_proj_kernel, grid=(m // tm,),
        in_specs=[pl.BlockSpec((tm, d), lambda i: (i, 0)), pl.BlockSpec((None, w, d), lambda i: (l, 0, 0))],
        out_specs=pl.BlockSpec((nb, tm // S5_CHUNK, S5_CHUNK * LANES), lambda i: (0, i, 0)),
        out_shape=jax.ShapeDtypeStruct((nb, m // S5_CHUNK, S5_CHUNK * LANES), BF16),
        scratch_shapes=[pltpu.VMEM((nb, tm, LANES), F32)],
        compiler_params=_params(1), name="s5_proj",
    )(u2, w_main_t)


def _s5_in_kernel(u_ref, w_ref, re_ref, im_ref):
    acc = jnp.dot(u_ref[0], w_ref[0], preferred_element_type=F32)
    w = re_ref.shape[2]
    re_ref[0] = acc[:, 0:w]
    re_ref[1] = acc[:, w:2 * w]
    im_ref[0] = acc[:, 2 * w:3 * w]
    im_ref[1] = acc[:, 3 * w:4 * w]


def _s5_scan_kernel(lre, lim, cre, cim, lr_ref, li_ref, o_lre, o_lim, o_cre, o_cim, *, n_batch, n_lat, n_ctx):
    d = pl.program_id(0)
    lr = lr_ref[0]
    li = li_ref[0]

    def run(re_ref, im_ref, ore_ref, oim_ref, n_chunks, carry):
        def body(j, carry):
            idx = jnp.where(d == 0, j, n_chunks - 1 - j)
            new = []
            for bi in range(n_batch):
                xr, xi = carry[bi]
                row = bi * n_chunks + idx
                ore_ref[0, pl.ds(row, 1), :] = xr
                oim_ref[0, pl.ds(row, 1), :] = xi
                ur = re_ref[0, pl.ds(row, 1), :]
                ui = im_ref[0, pl.ds(row, 1), :]
                new.append((lr * xr - li * xi + ur, lr * xi + li * xr + ui))
            return tuple(new)

        return lax.fori_loop(0, n_chunks, body, carry)

    zero = jnp.zeros(lr.shape, F32)
    carry = run(cre, cim, o_cre, o_cim, n_ctx, tuple((zero, zero) for _ in range(n_batch)))
    run(lre, lim, o_lre, o_lim, n_lat, carry)


def _s5_out_kernel(u_ref, xr_ref, xi_ref, t_ref, e_ref, o_ref):
    xcat = jnp.concatenate([xr_ref[0], xr_ref[1], xi_ref[0], xi_ref[1]], axis=1).astype(BF16)
    y = (jnp.dot(u_ref[0], t_ref[0], preferred_element_type=F32)
         + jnp.dot(xcat, e_ref[0], preferred_element_type=F32))
    o_ref[0] = _gelu_tanh(y).astype(o_ref.dtype)


def _s5_glu_kernel(y_ref, w_ref, b_ref, o_ref, g_ref):
    n_rows = y_ref.shape[1]
    for blk in range(y_ref.shape[0]):
        for s in range(S5_CHUNK):
            g_ref[blk, pl.ds(s, n_rows, stride=S5_CHUNK), :] = y_ref[blk, :, s * LANES:(s + 1) * LANES].astype(F32)
    g = jnp.concatenate([g_ref[blk] for blk in range(y_ref.shape[0])], axis=1)
    z = jnp.dot(g.astype(BF16), w_ref[...], preferred_element_type=F32) + b_ref[...]
    o_ref[...] = (g * jax.nn.sigmoid(z)).astype(o_ref.dtype)


def _s5_branch(uc_lat, uc_ctx, n_batch, tables, w_glu, b_glu, l, want_ctx):
    toep, ec, bc, lam_re, lam_im = tables
    nb, r_lat, cw = uc_lat.shape
    r_ctx = uc_ctx.shape[1]
    gn = lam_re.shape[2]
    sw = gn // nb
    w = nb * LANES

    def chunk_rows(r):
        return min(r, 512)

    def state_in(uc):
        r = uc.shape[1]
        rt = chunk_rows(r)
        x_spec = pl.BlockSpec((2, rt, sw), lambda k, i: (0, i, k))
        x_shape = jax.ShapeDtypeStruct((2, r, gn), F32)
        return pl.pallas_call(
            _s5_in_kernel, grid=(nb, r // rt),
            in_specs=[pl.BlockSpec((1, rt, cw), lambda k, i: (k, i, 0)),
                      pl.BlockSpec((1, cw, 4 * sw), lambda k, i: (k, 0, 0))],
            out_specs=[x_spec, x_spec], out_shape=[x_shape, x_shape],
            compiler_params=_params(2), name="s5_in",
        )(uc, bc)

    lre, lim = state_in(uc_lat)
    cre, cim = state_in(uc_ctx)
    lb = 512
    lat_blk = pl.BlockSpec((1, r_lat, lb), lambda d, j: (d, 0, j))
    ctx_blk = pl.BlockSpec((1, r_ctx, lb), lambda d, j: (d, 0, j))
    lam_spec = pl.BlockSpec((1, 1, lb), lambda d, j: (d, 0, j))
    lat_shape = jax.ShapeDtypeStruct((2, r_lat, gn), F32)
    ctx_shape = jax.ShapeDtypeStruct((2, r_ctx, gn), F32)
    xl_re, xl_im, xc_re, xc_im = pl.pallas_call(
        functools.partial(_s5_scan_kernel, n_batch=n_batch, n_lat=r_lat // n_batch, n_ctx=r_ctx // n_batch),
        grid=(2, gn // lb),
        in_specs=[lat_blk, lat_blk, ctx_blk, ctx_blk, lam_spec, lam_spec],
        out_specs=[lat_blk, lat_blk, ctx_blk, ctx_blk],
        out_shape=[lat_shape, lat_shape, ctx_shape, ctx_shape],
        compiler_params=_params(2), name="s5_scan",
    )(lre, lim, cre, cim, lam_re, lam_im)

    def readout(uc, x_re, x_im):
        r = uc.shape[1]
        rt = chunk_rows(r)
        hw = cw // 2
        x_spec = pl.BlockSpec((2, rt, sw), lambda k, h, i: (0, i, k))
        return pl.pallas_call(
            _s5_out_kernel, grid=(nb, 2, r // rt),
            in_specs=[pl.BlockSpec((1, rt, cw), lambda k, h, i: (k, i, 0)), x_spec, x_spec,
                      pl.BlockSpec((1, cw, hw), lambda k, h, i: (k, 0, h)),
                      pl.BlockSpec((1, 4 * sw, hw), lambda k, h, i: (k, 0, h))],
            out_specs=pl.BlockSpec((1, rt, hw), lambda k, h, i: (k, i, h)),
            out_shape=jax.ShapeDtypeStruct((nb, r, cw), BF16),
            compiler_params=_params(3), name="s5_out",
        )(uc, x_re, x_im, toep, ec)

    def glu(y):
        m = y.shape[1] * S5_CHUNK
        tm = min(m, ROW_TILE)
        return pl.pallas_call(
            _s5_glu_kernel, grid=(m // tm,),
            in_specs=[pl.BlockSpec((nb, tm // S5_CHUNK, cw), lambda i: (0, i, 0)),
                      pl.BlockSpec((None, w, w), lambda i: (l, 0, 0)),
                      pl.BlockSpec((None, 1, w), lambda i: (l, 0, 0))],
            out_specs=pl.BlockSpec((tm, w), lambda i: (i, 0)),
            out_shape=jax.ShapeDtypeStruct((m, w), BF16),
            scratch_shapes=[pltpu.VMEM((nb, tm, LANES), F32)],
            compiler_params=_params(1), name="s5_glu",
        )(y, w_glu, b_glu)

    out_lat = glu(readout(uc_lat, xl_re, xl_im))
    out_ctx = glu(readout(uc_ctx, xc_re, xc_im)) if want_ctx else None
    return out_lat, out_ctx


def _conv_kernel(hm, hp, hn, cm, cp, cn, bm, w_ref, o_ref, *, n_tiles):
    i = pl.program_id(1)
    xg = hm[0].astype(F32) * cm[0].astype(F32)
    tm = xg.shape[0]
    x_before = jnp.where(i > 0, hp[0, HALO - 1:HALO].astype(F32) * cp[0, HALO - 1:HALO].astype(F32), 0.0)
    x_after = jnp.where(i < n_tiles - 1, hn[0, 0:1].astype(F32) * cn[0, 0:1].astype(F32), 0.0)
    row = lax.broadcasted_iota(jnp.int32, xg.shape, 0)
    prev = jnp.where(row == 0, x_before, pltpu.roll(xg, 1, axis=0))
    nxt = jnp.where(row == tm - 1, x_after, pltpu.roll(xg, tm - 1, axis=0))
    w = w_ref[...]
    y = w[0:1] * prev + w[1:2] * xg + w[2:3] * nxt
    o_ref[0] = (bm[0].astype(F32) * y).astype(o_ref.dtype)


def _halo_specs(tm, width, col_block, n_halo_blocks):
    hb = tm // HALO
    main = pl.BlockSpec((1, tm, width), lambda b, i, c: (b, i, col_block + c))
    before = pl.BlockSpec((1, HALO, width), lambda b, i, c: (b, jnp.maximum(i * hb - 1, 0), col_block + c))
    after = pl.BlockSpec((1, HALO, width),
                         lambda b, i, c: (b, jnp.minimum((i + 1) * hb, n_halo_blocks - 1), col_block + c))
    return main, before, after


def _conv_branch(z, conv_w, l, col0):
    b, t, _ = z.shape
    w = conv_w.shape[2]
    tm = min(t, ROW_TILE)
    tc = 512
    n_tiles = t // tm
    cb = col0 // tc
    h_specs = _halo_specs(tm, tc, cb, t // HALO)
    bg_spec = _halo_specs(tm, tc, cb + w // tc, t // HALO)[0]
    c_specs = _halo_specs(tm, tc, cb + 2 * (w // tc), t // HALO)
    out = pl.pallas_call(
        functools.partial(_conv_kernel, n_tiles=n_tiles), grid=(b, n_tiles, w // tc),
        in_specs=[*h_specs, *c_specs, bg_spec, pl.BlockSpec((None, 3, tc), lambda bi, i, c: (l, 0, c))],
        out_specs=pl.BlockSpec((1, tm, tc), lambda bi, i, c: (bi, i, c)),
        out_shape=jax.ShapeDtypeStruct((b, t, w), BF16),
        compiler_params=_params(3), name="conv",
    )(z, z, z, z, z, z, z, conv_w)
    return out.reshape(b * t, w)


def _pool_kernel(um, up, un, w_ref, s_ref, o_ref, *, n_tiles, seq_len):
    i = pl.program_id(1)
    gi = pl.program_id(2)
    xm = um[0]
    tm = xm.shape[0]
    win = jnp.left_shift(POOL_WINDOWS[0], gi)
    half = win // 2
    x_before = jnp.where(i > 0, up[0], jnp.zeros_like(up[0]))
    x_after = jnp.where(i < n_tiles - 1, un[0], jnp.zeros_like(un[0]))
    ext = jnp.concatenate([x_before, xm, x_after], axis=0)
    s = lax.broadcasted_iota(jnp.int32, (tm, tm + 2 * HALO), 0)
    r = lax.broadcasted_iota(jnp.int32, (tm, tm + 2 * HALO), 1) - HALO
    off = r - s
    band = jnp.where(off >= -half, jnp.where(off < win - half, 1.0, 0.0), 0.0).astype(BF16)
    wsum = jnp.dot(band, ext, preferred_element_type=F32)
    t = i * tm + lax.broadcasted_iota(jnp.int32, (tm, 1), 0)
    cnt = (jnp.minimum(t + win - half, seq_len) - jnp.maximum(t - half, 0)).astype(F32)
    p = wsum / cnt - xm.astype(F32)
    y = jnp.dot(p.astype(BF16), w_ref[...], preferred_element_type=F32) * s_ref[...]
    o_ref[0] = y.astype(o_ref.dtype)


def _pool_branch(z, pool_w, pool_scale, l, col0):
    b, t, _ = z.shape
    n_groups, gw = pool_w.shape[1], pool_w.shape[2]
    tm = min(t, 512)
    n_tiles = t // tm
    specs = _halo_specs(tm, gw, col0 // gw, t // HALO)
    out = pl.pallas_call(
        functools.partial(_pool_kernel, n_tiles=n_tiles, seq_len=t), grid=(b, n_tiles, n_groups),
        in_specs=[*specs, pl.BlockSpec((None, None, gw, gw), lambda bi, i, c: (l, c, 0, 0)),
                  pl.BlockSpec((None, 1, gw), lambda bi, i, c: (l, 0, c))],
        out_specs=pl.BlockSpec((1, tm, gw), lambda bi, i, c: (bi, i, c)),
        out_shape=jax.ShapeDtypeStruct((b, t, n_groups * gw), BF16),
        compiler_params=_params(3), name="pool",
    )(z, z, z, pool_w, pool_scale)
    return out.reshape(b * t, n_groups * gw)


def _mlstm_chunk(q, k, v, li_col, lf_col, li_row, lf_row, state, reverse):
    c_mat, n_row, m = state
    lc = q.shape[0]
    ti = lax.broadcasted_iota(jnp.int32, (lc, lc), 0)
    si = lax.broadcasted_iota(jnp.int32, (lc, lc), 1)
    seen = (si >= ti) if reverse else (si <= ti)
    seen_t = (ti >= si) if reverse else (ti <= si)
    bcum_col = jnp.sum(jnp.where(seen, lf_row, 0.0), axis=1, keepdims=True)
    bcum_row = jnp.sum(jnp.where(seen_t, lf_col, 0.0), axis=0, keepdims=True)
    b_last = jnp.sum(lf_row, axis=1, keepdims=True)
    dmat = bcum_col - bcum_row + li_row
    m_inter = bcum_col + m
    m_t = jnp.maximum(jnp.max(jnp.where(seen, dmat, -1e30), axis=1, keepdims=True), m_inter)
    scores = lax.dot_general(q, k, (((1,), (1,)), ((), ())), preferred_element_type=F32)
    wgt = jnp.where(seen, scores * jnp.exp(dmat - m_t), 0.0)
    decay = jnp.exp(m_inter - m_t)
    num = (jnp.dot(wgt.astype(BF16), v, preferred_element_type=F32)
           + decay * jnp.dot(q, c_mat.astype(BF16), preferred_element_type=F32))
    den = (jnp.sum(wgt, axis=1, keepdims=True)
           + decay * jnp.sum(q.astype(F32) * n_row, axis=1, keepdims=True))
    h = num / jnp.maximum(jnp.abs(den), jnp.exp(-m_t))
    g_col = b_last - bcum_col + li_col
    m_new = jnp.maximum(b_last + m, jnp.max(g_col, axis=0, keepdims=True))
    carry = jnp.exp(b_last + m - m_new)
    wk = k.astype(F32) * jnp.exp(g_col - m_new)
    c_new = carry * c_mat + jnp.dot(wk.T.astype(BF16), v, preferred_element_type=F32)
    n_new = carry * n_row + jnp.sum(wk, axis=0, keepdims=True)
    return h, (c_new, n_new, m_new)


def _mlstm_kernel(ql, kl, vl, ol, gcl, grl, qc, kc, vc, oc, gcc, grc, gain_ref, out_l, out_c, h_l, h_c):
    lc = MLSTM_CHUNK
    dk, dv = ql.shape[2], vl.shape[2]

    def run(refs, h_ref, n_chunks, states):
        q_ref, k_ref, v_ref, gc_ref, gr_ref = refs

        def one(c, st, d):
            t0 = pl.multiple_of(c * lc, lc)
            gcol = gc_ref[0, 0, pl.ds(t0, lc), :]
            li_col = gcol[:, d:d + 1]
            lf_col = _log_sigmoid(gcol[:, 2 + d:3 + d])
            li_row = gr_ref[0, 0, d, pl.ds(c, 1), :]
            lf_row = _log_sigmoid(gr_ref[0, 0, 2 + d, pl.ds(c, 1), :])
            h, st = _mlstm_chunk(q_ref[0, pl.ds(t0, lc), :], k_ref[0, pl.ds(t0, lc), :],
                                 v_ref[0, pl.ds(t0, lc), :], li_col, lf_col, li_row, lf_row, st, d == 1)
            return t0, h, st

        def make_body(second_visit):
            def body(j, sts):
                t_f, h_f, st_f = one(j, sts[0], 0)
                t_b, h_b, st_b = one(n_chunks - 1 - j, sts[1], 1)
                for t0, h in ((t_f, h_f), (t_b, h_b)):
                    if second_visit:
                        h_ref[pl.ds(t0, lc), :] += h
                    else:
                        h_ref[pl.ds(t0, lc), :] = h
                return (st_f, st_b)
            return body

        half = n_chunks // 2
        states = lax.fori_loop(0, half, make_body(False), states)
        return lax.fori_loop(half, n_chunks, make_body(True), states)

    lat = (ql, kl, vl, gcl, grl)
    ctx = (qc, kc, vc, gcc, grc)
    n_lat, n_ctx = ql.shape[1] // lc, qc.shape[1] // lc
    zero = (jnp.zeros((dk, dv), F32), jnp.zeros((1, dk), F32), jnp.zeros((1, 1), F32))
    states = run(ctx, h_c, n_ctx, (zero, zero))
    run(lat, h_l, n_lat, states)

    gain = gain_ref[...]

    def readout(h_ref, o_ref, out_ref):
        t = h_ref.shape[0]
        tile = min(t, 512)

        def body(i, _):
            t0 = pl.multiple_of(i * tile, tile)
            h = h_ref[pl.ds(t0, tile), :]
            hn = h * lax.rsqrt(jnp.mean(h * h, axis=-1, keepdims=True) + EPS) * gain
            og = jax.nn.sigmoid(o_ref[0, pl.ds(t0, tile), :].astype(F32))
            out_ref[0, pl.ds(t0, tile), :] = (hn * og).astype(out_ref.dtype)
            return 0

        lax.fori_loop(0, t // tile, body, 0)

    readout(h_l, ol, out_l)
    readout(h_c, oc, out_c)


def _mlstm_branch(z_lat, zg_lat, z_ctx, zg_ctx, norm_gain, l, col_q, col_k, col_v, col_o):
    b, t, _ = z_lat.shape
    tc = z_ctx.shape[1]
    nh = MLSTM_HEADS
    w = norm_gain.shape[2]
    dv = w // nh
    dk = dv // 2
    lc = MLSTM_CHUNK
    assert (t // lc) % 2 == 0 and (tc // lc) % 2 == 0, "the two-direction loop pairs chunks"

    def gate_layouts(zg, tt):
        g = zg[:, :4 * nh].reshape(b, tt, 4, nh)
        col = g.transpose(0, 3, 1, 2)
        row = g.transpose(0, 3, 2, 1).reshape(b, nh, 4, tt // lc, lc)
        return col, row

    gcl, grl = gate_layouts(zg_lat, t)
    gcc, grc = gate_layouts(zg_ctx, tc)
    once = pl.Buffered(1)

    def specs(tt):
        return [pl.BlockSpec((1, tt, dk), lambda bi, h: (bi, 0, col_q // dk + h), pipeline_mode=once),
                pl.BlockSpec((1, tt, dk), lambda bi, h: (bi, 0, col_k // dk + h), pipeline_mode=once),
                pl.BlockSpec((1, tt, dv), lambda bi, h: (bi, 0, col_v // dv + h), pipeline_mode=once),
                pl.BlockSpec((1, tt, dv), lambda bi, h: (bi, 0, col_o // dv + h), pipeline_mode=once),
                pl.BlockSpec((1, 1, tt, 4), lambda bi, h: (bi, h, 0, 0), pipeline_mode=once),
                pl.BlockSpec((1, 1, 4, tt // lc, lc), lambda bi, h: (bi, h, 0, 0, 0))]

    out_l, out_c = pl.pallas_call(
        _mlstm_kernel, grid=(b, nh),
        in_specs=specs(t) + specs(tc) + [pl.BlockSpec((None, 1, dv), lambda bi, h: (l, 0, h))],
        out_specs=[pl.BlockSpec((1, t, dv), lambda bi, h: (bi, 0, h)),
                   pl.BlockSpec((1, tc, dv), lambda bi, h: (bi, 0, h))],
        out_shape=[jax.ShapeDtypeStruct((b, t, w), BF16), jax.ShapeDtypeStruct((b, tc, w), BF16)],
        scratch_shapes=[pltpu.VMEM((t, dv), F32), pltpu.VMEM((tc, dv), F32)],
        compiler_params=_params(2), name="mlstm",
    )(z_lat, z_lat, z_lat, z_lat, gcl, grl, z_ctx, z_ctx, z_ctx, z_ctx, gcc, grc, norm_gain)
    return out_l.reshape(b * t, w), out_c.reshape(b * tc, w)


def _token_mixer(u_lat, u_ctx, want_ctx, wts, l):
    b, t, d = u_lat.shape
    tc = u_ctx.shape[1]
    w = d // 4
    n_rest = 7 * w
    col_conv, col_pool, col_q, col_k, col_v, col_o = 0, 3 * w, 4 * w, 4 * w + w // 2, 5 * w, 6 * w

    def in_proj(u):
        m = u.shape[0] * u.shape[1]
        u2 = u.reshape(m, d)
        tm = min(m, ROW_TILE)
        us5 = _s5_proj(u2, wts['w_main_t'], l, w)
        z = _mm(u2, tm, w, n_rest, [_layer_wt_spec(l, d, w, 1)], [wts['w_main_t']], [], [],
                _plain_epilogue, BF16, "in_proj", w_transposed=True)
        zg = _mm(u2, tm, LANES, LANES, [_layer_wt_spec(l, d, LANES)], [wts['w_mgate_t']],
                 [pl.BlockSpec((None, 1, LANES), lambda i, j: (l, 0, 0))], [wts['mgate_bias']],
                 _bias_epilogue, F32, "in_proj_gates", w_transposed=True)
        return u2, us5, z.reshape(u.shape[0], u.shape[1], n_rest), zg

    u2_lat, us5_lat, z_lat, zg_lat = in_proj(u_lat)
    u2_ctx, us5_ctx, z_ctx, zg_ctx = in_proj(u_ctx)
    s5_l, s5_c = _s5_branch(us5_lat, us5_ctx, b, wts['s5_tables'][l],
                            wts['s5_w_glu'], wts['s5_b_glu'], l, want_ctx)
    ml_l, ml_c = _mlstm_branch(z_lat, zg_lat, z_ctx, zg_ctx, wts['mlstm_norm_gain'], l,
                               col_q, col_k, col_v, col_o)
    conv_l = _conv_branch(z_lat, wts['conv_w'], l, col_conv)
    pool_l = _pool_branch(z_lat, wts['pool_w'], wts['pool_scale'], l, col_pool)
    y_lat = _merge(u2_lat, (s5_l, conv_l, pool_l, ml_l), wts['w_mix_t'], wts['w_branch'], l)
    if not want_ctx:
        return y_lat, None
    conv_c = _conv_branch(z_ctx, wts['conv_w'], l, col_conv)
    pool_c = _pool_branch(z_ctx, wts['pool_w'], wts['pool_scale'], l, col_pool)
    y_ctx = _merge(u2_ctx, (s5_c, conv_c, pool_c, ml_c), wts['w_mix_t'], wts['w_branch'], l)
    return y_lat, y_ctx


def _out_proj_resid(h, y, gate, w_out, l):
    b, t, d = h.shape
    m = b * t
    tm = min(m, ROW_TILE)
    tn = 512
    out = _mm(y, tm, tn, d, [_layer_w_spec(l, d, tn)], [w_out],
              [pl.BlockSpec((tm, tn), lambda i, j: (i, j)), _gate_spec(gate, tm, t, tn)],
              [h.reshape(m, d), gate], functools.partial(_resid_epilogue, coef=1.0), F32, "out_proj_resid")
    return out.reshape(b, t, d)


def kernel(x, c, ctx, c_ctx, w_ada, b_ada, w_ffn1_in, w_ffn1_out, w_ffn2_in, w_ffn2_out, w_in, s5_a_re, s5_a_im, s5_log_dt, s5_b_re, s5_b_im, s5_c_re, s5_c_im, s5_d, s5_w_glu, s5_b_glu, conv_w, pool_w, pool_scale, mlstm_gate_bias, mlstm_norm_gain, w_branch, w_out, final_gain):
    n_batch, t_lat, d = x.shape
    depth = w_ada.shape[0]
    w = d // 4
    n_main = 8 * w
    n_mgate = 4 * MLSTM_HEADS
    dk = w // MLSTM_HEADS // 2

    w_in_t = jnp.transpose(w_in, (0, 2, 1))
    row_scale = jnp.ones((n_main, 1), F32).at[5 * w:5 * w + w // 2].set(dk ** -0.5)
    wts = {
        'w_main_t': (w_in_t[:, :n_main] * row_scale).astype(BF16),
        'w_mgate_t': jnp.pad(w_in_t[:, n_main:n_main + n_mgate], ((0, 0), (0, LANES - n_mgate), (0, 0))).astype(BF16),
        'mgate_bias': jnp.pad(mlstm_gate_bias.reshape(depth, 1, n_mgate).astype(F32),
                              ((0, 0), (0, 0), (0, LANES - n_mgate))),
        'w_mix_t': w_in_t[:, n_main + n_mgate:].astype(BF16),
        'w_branch': w_branch.astype(BF16),
        's5_w_glu': s5_w_glu.astype(BF16),
        's5_b_glu': s5_b_glu.astype(F32).reshape(depth, 1, w),
        'conv_w': conv_w.astype(F32),
        'pool_w': pool_w.astype(BF16),
        'pool_scale': pool_scale.astype(F32).reshape(depth, 1, w),
        'mlstm_norm_gain': mlstm_norm_gain.astype(F32).reshape(depth, 1, w),
        's5_tables': [_s5_tables(s5_a_re[l], s5_a_im[l], s5_log_dt[l], s5_b_re[l], s5_b_im[l],
                                 s5_c_re[l], s5_c_im[l], s5_d[l]) for l in range(depth)],
    }
    w1i, w1o = w_ffn1_in.astype(BF16), w_ffn1_out.astype(BF16)
    w2i, w2o = w_ffn2_in.astype(BF16), w_ffn2_out.astype(BF16)
    w_o = w_out.astype(BF16)

    c_all = jnp.zeros((8, d), F32).at[:n_batch].set(c).at[n_batch].set(c_ctx)
    mods = _ada_mod(c_all, w_ada, b_ada).reshape(depth, 8, N_MOD, d)

    h, hc = x, ctx
    for l in range(depth):
        last = l == depth - 1
        mod = [mods[l, :n_batch, k][:, None, :] for k in range(N_MOD)]
        modc = [mods[l, n_batch:n_batch + 1, k][:, None, :] for k in range(N_MOD)]
        h = _ffn(h, mod[0], mod[1], mod[2], w1i, w1o, l)
        hc = _ffn(hc, modc[0], modc[1], modc[2], w1i, w1o, l)
        col_major = (l % 2) == 1
        u = _prep(h, mod[3], mod[4])
        uc = _prep(hc, modc[3], modc[4])
        if col_major:
            u = u.reshape(n_batch, t_lat // GRID_W, GRID_W, d).transpose(0, 2, 1, 3).reshape(n_batch, t_lat, d)
        y, yc = _token_mixer(u, uc, not last, wts, l)
        if col_major:
            y = y.reshape(n_batch, GRID_W, t_lat // GRID_W, d).transpose(0, 2, 1, 3).reshape(n_batch * t_lat, d)
        h = _out_proj_resid(h, y, mod[5], w_o, l)
        h = _ffn(h, mod[6], mod[7], mod[8], w2i, w2o, l)
        if not last:
            hc = _out_proj_resid(hc, yc, modc[5], w_o, l)
            hc = _ffn(hc, modc[6], modc[7], modc[8], w2i, w2o, l)
    return _final_norm(h, final_gain)
```

```python
import functools
import math

import jax
import jax.numpy as jnp
from jax import lax
from jax.experimental import pallas as pl
from jax.experimental.pallas import tpu as pltpu

F32 = jnp.float32
BF16 = jnp.bfloat16
EPS = 1e-6
GRID_W = 64
N_MOD = 9
N_BRANCH = 4
S5_P = 16
S5_N = 64
S5_CHUNK = 16
LANES = 128
S5_LANE_GROUPS = LANES // S5_P
POOL_WINDOWS = (2, 4, 8, 16)
MLSTM_HEADS = 4
MLSTM_CHUNK = 64
BF16_ROWS = 16
HALO = BF16_ROWS
ROW_TILE = 1024
VMEM_LIMIT = 56 * 1024 * 1024
HI = lax.Precision.HIGHEST


def _params(n_grid_dims):
    return pltpu.CompilerParams(dimension_semantics=("arbitrary",) * n_grid_dims,
                                vmem_limit_bytes=VMEM_LIMIT)


def _gelu_tanh(y):
    return 0.5 * y * (1.0 + jnp.tanh(math.sqrt(2.0 / math.pi) * (y + 0.044715 * (y * y * y))))


def _log_sigmoid(x):
    return jnp.minimum(x, 0.0) - jnp.log1p(jnp.exp(-jnp.abs(x)))


def _ada_kernel(c_ref, w_ref, b_ref, o_ref):
    c = c_ref[...]
    a = (c * jax.nn.sigmoid(c)).astype(BF16)
    o_ref[0] = jnp.dot(a, w_ref[0].astype(BF16), preferred_element_type=F32) + b_ref[0]


def _ada_mod(c_all, w_ada, b_ada):
    n_layers, d, nd = w_ada.shape
    tn = 1024
    return pl.pallas_call(
        _ada_kernel,
        grid=(n_layers, nd // tn),
        in_specs=[pl.BlockSpec((8, d), lambda l, j: (0, 0)),
                  pl.BlockSpec((1, d, tn), lambda l, j: (l, 0, j)),
                  pl.BlockSpec((1, 1, tn), lambda l, j: (l, 0, j))],
        out_specs=pl.BlockSpec((1, 8, tn), lambda l, j: (l, 0, j)),
        out_shape=jax.ShapeDtypeStruct((n_layers, 8, nd), F32),
        compiler_params=_params(2), name="ada_mod",
    )(c_all, w_ada, b_ada.reshape(n_layers, 1, nd))


def _prep_kernel(h_ref, shift_ref, scale_ref, o_ref):
    x = h_ref[0]
    ms = jnp.mean(x * x, axis=-1, keepdims=True)
    o_ref[0] = (x * lax.rsqrt(ms + EPS) * (1.0 + scale_ref[0]) + shift_ref[0]).astype(o_ref.dtype)


def _prep(h, shift, scale):
    b, t, d = h.shape
    per_batch = shift.shape[0] > 1
    mod_spec = pl.BlockSpec((1, 1, d), lambda bi, i: (bi if per_batch else 0, 0, 0))
    tr = min(t, 512)
    spec = pl.BlockSpec((1, tr, d), lambda bi, i: (bi, i, 0))
    return pl.pallas_call(
        _prep_kernel, grid=(b, t // tr), in_specs=[spec, mod_spec, mod_spec], out_specs=spec,
        out_shape=jax.ShapeDtypeStruct((b, t, d), BF16),
        compiler_params=_params(2), name="prep",
    )(h, shift, scale)


def _final_kernel(h_ref, g_ref, o_ref):
    x = h_ref[0]
    ms = jnp.mean(x * x, axis=-1, keepdims=True)
    o_ref[0] = x * lax.rsqrt(ms + EPS) * g_ref[...]


def _final_norm(h, gain):
    b, t, d = h.shape
    tr = min(t, 512)
    spec = pl.BlockSpec((1, tr, d), lambda bi, i: (bi, i, 0))
    return pl.pallas_call(
        _final_kernel, grid=(b, t // tr),
        in_specs=[spec, pl.BlockSpec((1, d), lambda bi, i: (0, 0))], out_specs=spec,
        out_shape=jax.ShapeDtypeStruct((b, t, d), F32),
        compiler_params=_params(2), name="final_norm",
    )(h, gain.reshape(1, d))


_NT_DIMS = (((1,), (1,)), ((), ()))


def _mm_kernel(*refs, n_w, n_extra, epilogue, w_transposed):
    x_ref = refs[0]
    w_refs = refs[1:1 + n_w]
    extra = refs[1 + n_w:1 + n_w + n_extra]
    o_ref = refs[1 + n_w + n_extra]
    x = x_ref[...]
    if w_transposed:
        accs = [lax.dot_general(x, w[...], _NT_DIMS, preferred_element_type=F32) for w in w_refs]
    else:
        accs = [jnp.dot(x, w[...], preferred_element_type=F32) for w in w_refs]
    o_ref[...] = epilogue(accs, x, extra).astype(o_ref.dtype)


def _mm(x, tm, tn, n_cols, w_specs, w_arrays, extra_specs, extra_arrays, epilogue, out_dtype, name,
        w_transposed=False):
    m, k = x.shape
    return pl.pallas_call(
        functools.partial(_mm_kernel, n_w=len(w_arrays), n_extra=len(extra_arrays), epilogue=epilogue,
                          w_transposed=w_transposed),
        grid=(m // tm, n_cols // tn),
        in_specs=[pl.BlockSpec((tm, k), lambda i, j: (i, 0))] + list(w_specs) + list(extra_specs),
        out_specs=pl.BlockSpec((tm, tn), lambda i, j: (i, j)),
        out_shape=jax.ShapeDtypeStruct((m, n_cols), out_dtype),
        compiler_params=_params(2), name=name,
    )(x, *w_arrays, *extra_arrays)


def _layer_w_spec(l, k, tn, col_block_offset=0):
    return pl.BlockSpec((None, k, tn), lambda i, j: (l, 0, col_block_offset + j))


def _layer_wt_spec(l, k, tn, row_block_offset=0):
    return pl.BlockSpec((None, tn, k), lambda i, j: (l, row_block_offset + j, 0))


def _gate_spec(gate, tm, rows_per_batch, tn):
    per_batch = gate.shape[0] > 1
    return pl.BlockSpec((1, 1, tn), lambda i, j: ((i * tm) // rows_per_batch if per_batch else 0, 0, j))


def _swiglu_epilogue(accs, x, extra):
    a, g = accs
    return g * jax.nn.sigmoid(g) * a


def _resid_epilogue(accs, x, extra, *, coef):
    h_ref, gate_ref = extra
    return h_ref[...] + (coef * gate_ref[0]) * accs[0]


def _plain_epilogue(accs, x, extra):
    return accs[0]


def _bias_epilogue(accs, x, extra):
    return accs[0] + extra[0][...]


def _glu_epilogue(accs, x, extra):
    return x.astype(F32) * jax.nn.sigmoid(accs[0] + extra[0][...])


def _ffn(h, shift, scale, gate, w_in, w_out, l):
    b, t, d = h.shape
    m = b * t
    d_ff = w_out.shape[1]
    tm = min(m, ROW_TILE)
    tn = 512
    xn = _prep(h, shift, scale).reshape(m, d)
    act = _mm(xn, tm, tn, d_ff,
              [_layer_w_spec(l, d, tn), _layer_w_spec(l, d, tn, d_ff // tn)], [w_in, w_in],
              [], [], _swiglu_epilogue, BF16, "ffn_in")
    out = _mm(act, tm, tn, d,
              [_layer_w_spec(l, d_ff, tn)], [w_out],
              [pl.BlockSpec((tm, tn), lambda i, j: (i, j)), _gate_spec(gate, tm, t, tn)],
              [h.reshape(m, d), gate],
              functools.partial(_resid_epilogue, coef=0.5), F32, "ffn_out")
    return out.reshape(b, t, d)


def _merge_kernel(u_ref, b0, b1, b2, b3, wg_ref, wb_ref, o_ref, acc_ref):
    g = pl.program_id(2)
    gate = jax.nn.sigmoid(lax.dot_general(u_ref[...], wg_ref[0], _NT_DIMS, preferred_element_type=F32))
    for k, br in enumerate((b0, b1, b2, b3)):
        @pl.when(g == k)
        def _(k=k, br=br):
            val = gate * jnp.dot(br[...], wb_ref[...], preferred_element_type=F32)
            if k == 0:
                acc_ref[...] = val
            else:
                acc_ref[...] += val

    @pl.when(g == N_BRANCH - 1)
    def _():
        o_ref[...] = acc_ref[...].astype(o_ref.dtype)


def _merge(u, branches, w_in_t, gate_row0, w_branch, l):
    m, d = u.shape
    w = branches[0].shape[1]
    tm = min(m, ROW_TILE)
    tn = 512
    nj = d // tn
    assert gate_row0 % BF16_ROWS == 0, "gate rows must start on a packed sublane tile"
    br_spec = pl.BlockSpec((tm, w), lambda i, j, g: (i, 0))
    return pl.pallas_call(
        _merge_kernel, grid=(m // tm, nj, N_BRANCH),
        in_specs=[pl.BlockSpec((tm, d), lambda i, j, g: (i, 0)), br_spec, br_spec, br_spec, br_spec,
                  pl.BlockSpec((pl.Element(1), pl.Element(tn), pl.Element(d)),
                               lambda i, j, g: (l, pl.multiple_of(gate_row0 + (g * nj + j) * tn, BF16_ROWS), 0)),
                  pl.BlockSpec((None, None, w, tn), lambda i, j, g: (l, g, 0, j))],
        out_specs=pl.BlockSpec((tm, tn), lambda i, j, g: (i, j)),
        out_shape=jax.ShapeDtypeStruct((m, d), BF16),
        scratch_shapes=[pltpu.VMEM((tm, tn), F32)],
        compiler_params=_params(3), name="merge",
    )(u, *branches, w_in_t, w_branch)


def _s5_tables(a_re, a_im, log_dt, b_re, b_im, c_re, c_im, d_skip):
    a_re, a_im = a_re.astype(F32), a_im.astype(F32)
    n_groups = a_re.shape[1]
    lc, p, n = S5_CHUNK, S5_P, S5_N
    dt = jnp.exp(log_dt.astype(F32))[:, :, None]
    ks = jnp.arange(lc + 1, dtype=F32)[:, None, None, None]
    mag = jnp.exp(ks * (dt * a_re))
    pw_re, pw_im = mag * jnp.cos(ks * (dt * a_im)), mag * jnp.sin(ks * (dt * a_im))
    lam_re, lam_im = pw_re[1], pw_im[1]
    den = a_re * a_re + a_im * a_im
    z_re = ((lam_re - 1.0) * a_re + lam_im * a_im) / den
    z_im = (lam_im * a_re - (lam_re - 1.0) * a_im) / den
    b_re, b_im = b_re.astype(F32), b_im.astype(F32)
    bb_re = z_re[..., None] * b_re - z_im[..., None] * b_im
    bb_im = z_re[..., None] * b_im + z_im[..., None] * b_re
    c_re, c_im = c_re.astype(F32), c_im.astype(F32)
    cl_re = c_re[None] * pw_re[:, :, :, None, :] - c_im[None] * pw_im[:, :, :, None, :]
    cl_im = c_re[None] * pw_im[:, :, :, None, :] + c_im[None] * pw_re[:, :, :, None, :]
    kern = (jnp.einsum('tdgpn,dgnq->tdgpq', cl_re[:lc], bb_re, precision=HI)
            - jnp.einsum('tdgpn,dgnq->tdgpq', cl_im[:lc], bb_im, precision=HI))
    r_idx = jnp.arange(lc)[:, None]
    s_idx = jnp.arange(lc)[None, :]
    fwd = jnp.where((s_idx >= r_idx)[:, :, None, None, None], kern[jnp.clip(s_idx - r_idx, 0, lc - 1), 0], 0.0)
    bwd = jnp.where((r_idx >= s_idx)[:, :, None, None, None], kern[jnp.clip(r_idx - s_idx, 0, lc - 1), 1], 0.0)
    skip = (jnp.eye(lc, dtype=F32)[:, :, None, None, None]
            * (jnp.eye(p, dtype=F32)[None, None, None] * d_skip.astype(F32).reshape(1, 1, n_groups, p, 1)))
    gl = S5_LANE_GROUPS
    nb = n_groups // gl
    toep = (fwd + bwd + skip).reshape(lc, lc, nb, gl, p, p)
    toep = toep.transpose(2, 0, 3, 5, 1, 4).reshape(nb, lc * gl * p, lc * p)
    e_idx = jnp.stack([jnp.arange(1, lc + 1), jnp.arange(lc, 0, -1)])
    dsel = jnp.arange(2)[:, None]
    ec = jnp.stack([cl_re[e_idx, dsel], -cl_im[e_idx, dsel]])
    ec = ec.reshape(2, 2, lc, nb, gl, p, n).transpose(3, 0, 1, 4, 6, 2, 5)
    ec = ec.reshape(nb, 4 * gl * n, lc * p)
    f_idx = jnp.stack([jnp.arange(lc - 1, -1, -1), jnp.arange(lc)])
    fp_re, fp_im = pw_re[f_idx, dsel], pw_im[f_idx, dsel]
    bc_re = fp_re[..., None] * bb_re[:, None] - fp_im[..., None] * bb_im[:, None]
    bc_im = fp_re[..., None] * bb_im[:, None] + fp_im[..., None] * bb_re[:, None]
    bc = jnp.stack([bc_re, bc_im]).reshape(2, 2, lc, nb, gl, n, p)
    bc = bc.transpose(3, 2, 4, 6, 0, 1, 5).reshape(nb, lc * gl * p, 4 * n)
    lam_c_re = pw_re[lc].reshape(2, 1, n_groups * n)
    lam_c_im = pw_im[lc].reshape(2, 1, n_groups * n)
    return toep.astype(BF16), ec.astype(BF16), bc.astype(BF16), lam_c_re, lam_c_im


def _s5_expand_kernel(c_ref, sel_ref, o_ref, *, row_div, col_div):
    j = pl.program_id(1)
    spread = jnp.dot(c_ref[0], sel_ref[...], preferred_element_type=F32)
    rows, cols = spread.shape
    row_g = (lax.broadcasted_iota(jnp.int32, (rows, cols), 0) // row_div) % S5_LANE_GROUPS
    col_g = ((lax.broadcasted_iota(jnp.int32, (rows, cols), 1) + j * cols) // col_div) % S5_LANE_GROUPS
    o_ref[0] = jnp.where(row_g == col_g, spread, 0.0).astype(o_ref.dtype)


def _s5_expand(compact, inner, row_div):
    nblk, rows, cc = compact.shape
    gl = S5_LANE_GROUPS
    wide = cc * gl
    src = jnp.arange(wide)
    src = (src // (gl * inner)) * inner + src % inner
    sel = (jnp.arange(cc)[:, None] == src[None, :]).astype(BF16)
    tn = 512
    return pl.pallas_call(
        functools.partial(_s5_expand_kernel, row_div=row_div, col_div=inner), grid=(nblk, wide // tn),
        in_specs=[pl.BlockSpec((1, rows, cc), lambda k, j: (k, 0, 0)), pl.BlockSpec((cc, tn), lambda k, j: (0, j))],
        out_specs=pl.BlockSpec((1, rows, tn), lambda k, j: (k, 0, j)),
        out_shape=jax.ShapeDtypeStruct((nblk, rows, wide), BF16),
        compiler_params=_params(2), name="s5_expand",
    )(compact, sel)


def _s5_all_tables(a_re, a_im, log_dt, b_re, b_im, c_re, c_im, d_skip):
    toep, ec, bc, lam_re, lam_im = jax.vmap(_s5_tables)(a_re, a_im, log_dt, b_re, b_im, c_re, c_im, d_skip)
    depth, nb = toep.shape[:2]

    def expand(t, inner, row_div):
        big = _s5_expand(t.reshape((depth * nb,) + t.shape[2:]), inner, row_div)
        return big.reshape((depth, nb) + big.shape[1:])

    return (expand(toep, S5_P, S5_P),
            expand(ec, S5_P, S5_N),
            expand(bc, S5_N, S5_P),
            lam_re, lam_im)


def _s5_proj_kernel(x_ref, w_ref, o_ref, acc_ref):
    acc = lax.dot_general(x_ref[...], w_ref[...], _NT_DIMS, preferred_element_type=F32)
    n_rows = o_ref.shape[1]
    for blk in range(o_ref.shape[0]):
        acc_ref[blk] = acc[:, blk * LANES:(blk + 1) * LANES]
        for r in range(S5_CHUNK):
            rows = acc_ref[blk, pl.ds(r, n_rows, stride=S5_CHUNK), :]
            o_ref[blk, :, r * LANES:(r + 1) * LANES] = rows.astype(o_ref.dtype)


def _s5_proj(u2, w_main_t, l, w):
    m, d = u2.shape
    tm = min(m, ROW_TILE)
    nb = w // LANES
    return pl.pallas_call(
        _s5_proj_kernel, grid=(m // tm,),
        in_specs=[pl.BlockSpec((tm, d), lambda i: (i, 0)), pl.BlockSpec((None, w, d), lambda i: (l, 0, 0))],
        out_specs=pl.BlockSpec((nb, tm // S5_CHUNK, S5_CHUNK * LANES), lambda i: (0, i, 0)),
        out_shape=jax.ShapeDtypeStruct((nb, m // S5_CHUNK, S5_CHUNK * LANES), BF16),
        scratch_shapes=[pltpu.VMEM((nb, tm, LANES), F32)],
        compiler_params=_params(1), name="s5_proj",
    )(u2, w_main_t)


def _s5_in_kernel(u_ref, w_ref, re_ref, im_ref):
    acc = jnp.dot(u_ref[0], w_ref[0], preferred_element_type=F32)
    w = re_ref.shape[2]
    re_ref[0] = acc[:, 0:w]
    re_ref[1] = acc[:, w:2 * w]
    im_ref[0] = acc[:, 2 * w:3 * w]
    im_ref[1] = acc[:, 3 * w:4 * w]


def _s5_scan_kernel(lre, lim, cre, cim, lr_ref, li_ref, o_lre, o_lim, o_cre, o_cim, *, n_batch, n_lat, n_ctx):
    d = pl.program_id(0)
    lr = lr_ref[0]
    li = li_ref[0]

    def run(re_ref, im_ref, ore_ref, oim_ref, n_chunks, carry):
        def body(j, carry):
            idx = jnp.where(d == 0, j, n_chunks - 1 - j)
            new = []
            for bi in range(n_batch):
                xr, xi = carry[bi]
                row = bi * n_chunks + idx
                ore_ref[0, pl.ds(row, 1), :] = xr
                oim_ref[0, pl.ds(row, 1), :] = xi
                ur = re_ref[0, pl.ds(row, 1), :]
                ui = im_ref[0, pl.ds(row, 1), :]
                new.append((lr * xr - li * xi + ur, lr * xi + li * xr + ui))
            return tuple(new)

        return lax.fori_loop(0, n_chunks, body, carry)

    zero = jnp.zeros(lr.shape, F32)
    carry = run(cre, cim, o_cre, o_cim, n_ctx, tuple((zero, zero) for _ in range(n_batch)))
    run(lre, lim, o_lre, o_lim, n_lat, carry)


def _s5_out_kernel(u_ref, xr_ref, xi_ref, t_ref, e_ref, o_ref):
    xcat = jnp.concatenate([xr_ref[0], xr_ref[1], xi_ref[0], xi_ref[1]], axis=1).astype(BF16)
    y = (jnp.dot(u_ref[0], t_ref[0], preferred_element_type=F32)
         + jnp.dot(xcat, e_ref[0], preferred_element_type=F32))
    o_ref[0] = _gelu_tanh(y).astype(o_ref.dtype)


def _s5_glu_kernel(y_ref, w_ref, b_ref, o_ref, g_ref):
    n_rows = y_ref.shape[1]
    for blk in range(y_ref.shape[0]):
        for s in range(S5_CHUNK):
            g_ref[blk, pl.ds(s, n_rows, stride=S5_CHUNK), :] = y_ref[blk, :, s * LANES:(s + 1) * LANES].astype(F32)
    g = jnp.concatenate([g_ref[blk] for blk in range(y_ref.shape[0])], axis=1)
    z = jnp.dot(g.astype(BF16), w_ref[...], preferred_element_type=F32) + b_ref[...]
    o_ref[...] = (g * jax.nn.sigmoid(z)).astype(o_ref.dtype)


def _s5_branch(uc_lat, uc_ctx, n_batch, tables, w_glu, b_glu, l, want_ctx):
    toep, ec, bc, lam_re, lam_im = tables
    nb, r_lat, cw = uc_lat.shape
    r_ctx = uc_ctx.shape[1]
    gn = lam_re.shape[3]
    sw = gn // nb
    w = nb * LANES

    def chunk_rows(r):
        return min(r, 512)

    def state_in(uc):
        r = uc.shape[1]
        rt = chunk_rows(r)
        x_spec = pl.BlockSpec((2, rt, sw), lambda k, i: (0, i, k))
        x_shape = jax.ShapeDtypeStruct((2, r, gn), F32)
        return pl.pallas_call(
            _s5_in_kernel, grid=(nb, r // rt),
            in_specs=[pl.BlockSpec((1, rt, cw), lambda k, i: (k, i, 0)),
                      pl.BlockSpec((None, 1, cw, 4 * sw), lambda k, i: (l, k, 0, 0))],
            out_specs=[x_spec, x_spec], out_shape=[x_shape, x_shape],
            compiler_params=_params(2), name="s5_in",
        )(uc, bc)

    lre, lim = state_in(uc_lat)
    cre, cim = state_in(uc_ctx)
    lb = 512
    lat_blk = pl.BlockSpec((1, r_lat, lb), lambda d, j: (d, 0, j))
    ctx_blk = pl.BlockSpec((1, r_ctx, lb), lambda d, j: (d, 0, j))
    lam_spec = pl.BlockSpec((None, 1, 1, lb), lambda d, j: (l, d, 0, j))
    lat_shape = jax.ShapeDtypeStruct((2, r_lat, gn), F32)
    ctx_shape = jax.ShapeDtypeStruct((2, r_ctx, gn), F32)
    xl_re, xl_im, xc_re, xc_im = pl.pallas_call(
        functools.partial(_s5_scan_kernel, n_batch=n_batch, n_lat=r_lat // n_batch, n_ctx=r_ctx // n_batch),
        grid=(2, gn // lb),
        in_specs=[lat_blk, lat_blk, ctx_blk, ctx_blk, lam_spec, lam_spec],
        out_specs=[lat_blk, lat_blk, ctx_blk, ctx_blk],
        out_shape=[lat_shape, lat_shape, ctx_shape, ctx_shape],
        compiler_params=_params(2), name="s5_scan",
    )(lre, lim, cre, cim, lam_re, lam_im)

    def readout(uc, x_re, x_im):
        r = uc.shape[1]
        rt = chunk_rows(r)
        hw = cw // 2
        x_spec = pl.BlockSpec((2, rt, sw), lambda k, h, i: (0, i, k))
        return pl.pallas_call(
            _s5_out_kernel, grid=(nb, 2, r // rt),
            in_specs=[pl.BlockSpec((1, rt, cw), lambda k, h, i: (k, i, 0)), x_spec, x_spec,
                      pl.BlockSpec((None, 1, cw, hw), lambda k, h, i: (l, k, 0, h)),
                      pl.BlockSpec((None, 1, 4 * sw, hw), lambda k, h, i: (l, k, 0, h))],
            out_specs=pl.BlockSpec((1, rt, hw), lambda k, h, i: (k, i, h)),
            out_shape=jax.ShapeDtypeStruct((nb, r, cw), BF16),
            compiler_params=_params(3), name="s5_out",
        )(uc, x_re, x_im, toep, ec)

    def glu(y):
        m = y.shape[1] * S5_CHUNK
        tm = min(m, ROW_TILE)
        return pl.pallas_call(
            _s5_glu_kernel, grid=(m // tm,),
            in_specs=[pl.BlockSpec((nb, tm // S5_CHUNK, cw), lambda i: (0, i, 0)),
                      pl.BlockSpec((None, w, w), lambda i: (l, 0, 0)),
                      pl.BlockSpec((None, 1, w), lambda i: (l, 0, 0))],
            out_specs=pl.BlockSpec((tm, w), lambda i: (i, 0)),
            out_shape=jax.ShapeDtypeStruct((m, w), BF16),
            scratch_shapes=[pltpu.VMEM((nb, tm, LANES), F32)],
            compiler_params=_params(1), name="s5_glu",
        )(y, w_glu, b_glu)

    out_lat = glu(readout(uc_lat, xl_re, xl_im))
    out_ctx = glu(readout(uc_ctx, xc_re, xc_im)) if want_ctx else None
    return out_lat, out_ctx


def _conv_kernel(hm, hp, hn, cm, cp, cn, bm, w_ref, o_ref, *, n_tiles):
    i = pl.program_id(1)
    xg = hm[0].astype(F32) * cm[0].astype(F32)
    tm = xg.shape[0]
    x_before = jnp.where(i > 0, hp[0, HALO - 1:HALO].astype(F32) * cp[0, HALO - 1:HALO].astype(F32), 0.0)
    x_after = jnp.where(i < n_tiles - 1, hn[0, 0:1].astype(F32) * cn[0, 0:1].astype(F32), 0.0)
    row = lax.broadcasted_iota(jnp.int32, xg.shape, 0)
    prev = jnp.where(row == 0, x_before, pltpu.roll(xg, 1, axis=0))
    nxt = jnp.where(row == tm - 1, x_after, pltpu.roll(xg, tm - 1, axis=0))
    w = w_ref[...]
    y = w[0:1] * prev + w[1:2] * xg + w[2:3] * nxt
    o_ref[0] = (bm[0].astype(F32) * y).astype(o_ref.dtype)


def _halo_specs(tm, width, col_block, n_halo_blocks):
    hb = tm // HALO
    main = pl.BlockSpec((1, tm, width), lambda b, i, c: (b, i, col_block + c))
    before = pl.BlockSpec((1, HALO, width), lambda b, i, c: (b, jnp.maximum(i * hb - 1, 0), col_block + c))
    after = pl.BlockSpec((1, HALO, width),
                         lambda b, i, c: (b, jnp.minimum((i + 1) * hb, n_halo_blocks - 1), col_block + c))
    return main, before, after


def _conv_branch(z, conv_w, l, col0):
    b, t, _ = z.shape
    w = conv_w.shape[2]
    tm = min(t, ROW_TILE)
    tc = 512
    n_tiles = t // tm
    cb = col0 // tc
    h_specs = _halo_specs(tm, tc, cb, t // HALO)
    bg_spec = _halo_specs(tm, tc, cb + w // tc, t // HALO)[0]
    c_specs = _halo_specs(tm, tc, cb + 2 * (w // tc), t // HALO)
    out = pl.pallas_call(
        functools.partial(_conv_kernel, n_tiles=n_tiles), grid=(b, n_tiles, w // tc),
        in_specs=[*h_specs, *c_specs, bg_spec, pl.BlockSpec((None, 3, tc), lambda bi, i, c: (l, 0, c))],
        out_specs=pl.BlockSpec((1, tm, tc), lambda bi, i, c: (bi, i, c)),
        out_shape=jax.ShapeDtypeStruct((b, t, w), BF16),
        compiler_params=_params(3), name="conv",
    )(z, z, z, z, z, z, z, conv_w)
    return out.reshape(b * t, w)


def _pool_kernel(um, up, un, w_ref, s_ref, o_ref, *, n_tiles, seq_len):
    i = pl.program_id(1)
    gi = pl.program_id(2)
    xm = um[0]
    tm = xm.shape[0]
    win = jnp.left_shift(POOL_WINDOWS[0], gi)
    half = win // 2
    x_before = jnp.where(i > 0, up[0], jnp.zeros_like(up[0]))
    x_after = jnp.where(i < n_tiles - 1, un[0], jnp.zeros_like(un[0]))
    ext = jnp.concatenate([x_before, xm, x_after], axis=0)
    s = lax.broadcasted_iota(jnp.int32, (tm, tm + 2 * HALO), 0)
    r = lax.broadcasted_iota(jnp.int32, (tm, tm + 2 * HALO), 1) - HALO
    off = r - s
    band = jnp.where(off >= -half, jnp.where(off < win - half, 1.0, 0.0), 0.0).astype(BF16)
    wsum = jnp.dot(band, ext, preferred_element_type=F32)
    t = i * tm + lax.broadcasted_iota(jnp.int32, (tm, 1), 0)
    cnt = (jnp.minimum(t + win - half, seq_len) - jnp.maximum(t - half, 0)).astype(F32)
    p = wsum / cnt - xm.astype(F32)
    y = jnp.dot(p.astype(BF16), w_ref[...], preferred_element_type=F32) * s_ref[...]
    o_ref[0] = y.astype(o_ref.dtype)


def _pool_branch(z, pool_w, pool_scale, l, col0):
    b, t, _ = z.shape
    n_groups, gw = pool_w.shape[1], pool_w.shape[2]
    tm = min(t, 512)
    n_tiles = t // tm
    specs = _halo_specs(tm, gw, col0 // gw, t // HALO)
    out = pl.pallas_call(
        functools.partial(_pool_kernel, n_tiles=n_tiles, seq_len=t), grid=(b, n_tiles, n_groups),
        in_specs=[*specs, pl.BlockSpec((None, None, gw, gw), lambda bi, i, c: (l, c, 0, 0)),
                  pl.BlockSpec((None, 1, gw), lambda bi, i, c: (l, 0, c))],
        out_specs=pl.BlockSpec((1, tm, gw), lambda bi, i, c: (bi, i, c)),
        out_shape=jax.ShapeDtypeStruct((b, t, n_groups * gw), BF16),
        compiler_params=_params(3), name="pool",
    )(z, z, z, pool_w, pool_scale)
    return out.reshape(b * t, n_groups * gw)


def _mlstm_chunk(q, k, v, li_col, lf_col, li_row, lf_row, state, reverse):
    c_mat, n_row, m = state
    lc = q.shape[0]
    ti = lax.broadcasted_iota(jnp.int32, (lc, lc), 0)
    si = lax.broadcasted_iota(jnp.int32, (lc, lc), 1)
    seen = (si >= ti) if reverse else (si <= ti)
    seen_t = (ti >= si) if reverse else (ti <= si)
    bcum_col = jnp.sum(jnp.where(seen, lf_row, 0.0), axis=1, keepdims=True)
    bcum_row = jnp.sum(jnp.where(seen_t, lf_col, 0.0), axis=0, keepdims=True)
    b_last = jnp.sum(lf_row, axis=1, keepdims=True)
    dmat = bcum_col - bcum_row + li_row
    m_inter = bcum_col + m
    m_t = jnp.maximum(jnp.max(jnp.where(seen, dmat, -1e30), axis=1, keepdims=True), m_inter)
    scores = lax.dot_general(q, k, (((1,), (1,)), ((), ())), preferred_element_type=F32)
    wgt = jnp.where(seen, scores * jnp.exp(dmat - m_t), 0.0)
    decay = jnp.exp(m_inter - m_t)
    num = (jnp.dot(wgt.astype(BF16), v, preferred_element_type=F32)
           + decay * jnp.dot(q, c_mat.astype(BF16), preferred_element_type=F32))
    den = (jnp.sum(wgt, axis=1, keepdims=True)
           + decay * jnp.sum(q.astype(F32) * n_row, axis=1, keepdims=True))
    h = num / jnp.maximum(jnp.abs(den), jnp.exp(-m_t))
    g_col = b_last - bcum_col + li_col
    m_new = jnp.maximum(b_last + m, jnp.max(g_col, axis=0, keepdims=True))
    carry = jnp.exp(b_last + m - m_new)
    wk = k.astype(F32) * jnp.exp(g_col - m_new)
    c_new = carry * c_mat + jnp.dot(wk.T.astype(BF16), v, preferred_element_type=F32)
    n_new = carry * n_row + jnp.sum(wk, axis=0, keepdims=True)
    return h, (c_new, n_new, m_new)


def _mlstm_kernel(ql, kl, vl, ol, gcl, grl, qc, kc, vc, oc, gcc, grc, gain_ref, out_l, out_c, h_l, h_c):
    lc = MLSTM_CHUNK
    dk, dv = ql.shape[2], vl.shape[2]

    def run(refs, h_ref, n_chunks, states):
        q_ref, k_ref, v_ref, gc_ref, gr_ref = refs

        def one(c, st, d):
            t0 = pl.multiple_of(c * lc, lc)
            gcol = gc_ref[0, 0, pl.ds(t0, lc), :]
            li_col = gcol[:, d:d + 1]
            lf_col = _log_sigmoid(gcol[:, 2 + d:3 + d])
            li_row = gr_ref[0, 0, d, pl.ds(c, 1), :]
            lf_row = _log_sigmoid(gr_ref[0, 0, 2 + d, pl.ds(c, 1), :])
            h, st = _mlstm_chunk(q_ref[0, pl.ds(t0, lc), :], k_ref[0, pl.ds(t0, lc), :],
                                 v_ref[0, pl.ds(t0, lc), :], li_col, lf_col, li_row, lf_row, st, d == 1)
            return t0, h, st

        def make_body(second_visit):
            def body(j, sts):
                t_f, h_f, st_f = one(j, sts[0], 0)
                t_b, h_b, st_b = one(n_chunks - 1 - j, sts[1], 1)
                for t0, h in ((t_f, h_f), (t_b, h_b)):
                    if second_visit:
                        h_ref[pl.ds(t0, lc), :] += h
                    else:
                        h_ref[pl.ds(t0, lc), :] = h
                return (st_f, st_b)
            return body

        half = n_chunks // 2
        states = lax.fori_loop(0, half, make_body(False), states)
        return lax.fori_loop(half, n_chunks, make_body(True), states)

    lat = (ql, kl, vl, gcl, grl)
    ctx = (qc, kc, vc, gcc, grc)
    n_lat, n_ctx = ql.shape[1] // lc, qc.shape[1] // lc
    zero = (jnp.zeros((dk, dv), F32), jnp.zeros((1, dk), F32), jnp.zeros((1, 1), F32))
    states = run(ctx, h_c, n_ctx, (zero, zero))
    run(lat, h_l, n_lat, states)

    gain = gain_ref[...]

    def readout(h_ref, o_ref, out_ref):
        t = h_ref.shape[0]
        tile = min(t, 512)

        def body(i, _):
            t0 = pl.multiple_of(i * tile, tile)
            h = h_ref[pl.ds(t0, tile), :]
            hn = h * lax.rsqrt(jnp.mean(h * h, axis=-1, keepdims=True) + EPS) * gain
            og = jax.nn.sigmoid(o_ref[0, pl.ds(t0, tile), :].astype(F32))
            out_ref[0, pl.ds(t0, tile), :] = (hn * og).astype(out_ref.dtype)
            return 0

        lax.fori_loop(0, t // tile, body, 0)

    readout(h_l, ol, out_l)
    readout(h_c, oc, out_c)


def _mlstm_branch(z_lat, zg_lat, z_ctx, zg_ctx, norm_gain, l, col_q, col_k, col_v, col_o):
    b, t, _ = z_lat.shape
    tc = z_ctx.shape[1]
    nh = MLSTM_HEADS
    w = norm_gain.shape[2]
    dv = w // nh
    dk = dv // 2
    lc = MLSTM_CHUNK
    assert (t // lc) % 2 == 0 and (tc // lc) % 2 == 0, "the two-direction loop pairs chunks"

    def gate_layouts(zg, tt):
        g = zg[:, :4 * nh].reshape(b, tt, 4, nh)
        col = g.transpose(0, 3, 1, 2)
        row = g.transpose(0, 3, 2, 1).reshape(b, nh, 4, tt // lc, lc)
        return col, row

    gcl, grl = gate_layouts(zg_lat, t)
    gcc, grc = gate_layouts(zg_ctx, tc)
    once = pl.Buffered(1)

    def specs(tt):
        return [pl.BlockSpec((1, tt, dk), lambda bi, h: (bi, 0, col_q // dk + h), pipeline_mode=once),
                pl.BlockSpec((1, tt, dk), lambda bi, h: (bi, 0, col_k // dk + h), pipeline_mode=once),
                pl.BlockSpec((1, tt, dv), lambda bi, h: (bi, 0, col_v // dv + h), pipeline_mode=once),
                pl.BlockSpec((1, tt, dv), lambda bi, h: (bi, 0, col_o // dv + h), pipeline_mode=once),
                pl.BlockSpec((1, 1, tt, 4), lambda bi, h: (bi, h, 0, 0), pipeline_mode=once),
                pl.BlockSpec((1, 1, 4, tt // lc, lc), lambda bi, h: (bi, h, 0, 0, 0))]

    out_l, out_c = pl.pallas_call(
        _mlstm_kernel, grid=(b, nh),
        in_specs=specs(t) + specs(tc) + [pl.BlockSpec((None, 1, dv), lambda bi, h: (l, 0, h))],
        out_specs=[pl.BlockSpec((1, t, dv), lambda bi, h: (bi, 0, h)),
                   pl.BlockSpec((1, tc, dv), lambda bi, h: (bi, 0, h))],
        out_shape=[jax.ShapeDtypeStruct((b, t, w), BF16), jax.ShapeDtypeStruct((b, tc, w), BF16)],
        scratch_shapes=[pltpu.VMEM((t, dv), F32), pltpu.VMEM((tc, dv), F32)],
        compiler_params=_params(2), name="mlstm",
    )(z_lat, z_lat, z_lat, z_lat, gcl, grl, z_ctx, z_ctx, z_ctx, z_ctx, gcc, grc, norm_gain)
    return out_l.reshape(b * t, w), out_c.reshape(b * tc, w)


def _token_mixer(u_lat, u_ctx, want_ctx, wts, l):
    b, t, d = u_lat.shape
    tc = u_ctx.shape[1]
    w = d // 4
    n_rest = 7 * w
    col_conv, col_pool, col_q, col_k, col_v, col_o = 0, 3 * w, 4 * w, 4 * w + w // 2, 5 * w, 6 * w

    def in_proj(u):
        m = u.shape[0] * u.shape[1]
        u2 = u.reshape(m, d)
        tm = min(m, ROW_TILE)
        us5 = _s5_proj(u2, wts['w_in_t'], l, w)
        z = _mm(u2, tm, w, n_rest, [_layer_wt_spec(l, d, w, 1)], [wts['w_in_t']], [], [],
                _plain_epilogue, BF16, "in_proj", w_transposed=True)
        zg = _mm(u2, tm, LANES, LANES, [_layer_wt_spec(l, d, LANES)], [wts['w_mgate_t']],
                 [pl.BlockSpec((None, 1, LANES), lambda i, j: (l, 0, 0))], [wts['mgate_bias']],
                 _bias_epilogue, F32, "in_proj_gates", w_transposed=True)
        return u2, us5, z.reshape(u.shape[0], u.shape[1], n_rest), zg

    u2_lat, us5_lat, z_lat, zg_lat = in_proj(u_lat)
    u2_ctx, us5_ctx, z_ctx, zg_ctx = in_proj(u_ctx)
    s5_l, s5_c = _s5_branch(us5_lat, us5_ctx, b, wts['s5_tables'],
                            wts['s5_w_glu'], wts['s5_b_glu'], l, want_ctx)
    ml_l, ml_c = _mlstm_branch(z_lat, zg_lat, z_ctx, zg_ctx, wts['mlstm_norm_gain'], l,
                               col_q, col_k, col_v, col_o)
    conv_l = _conv_branch(z_lat, wts['conv_w'], l, col_conv)
    pool_l = _pool_branch(z_lat, wts['pool_w'], wts['pool_scale'], l, col_pool)
    y_lat = _merge(u2_lat, (s5_l, conv_l, pool_l, ml_l), wts['w_in_t'], wts['gate_row0'], wts['w_branch'], l)
    if not want_ctx:
        return y_lat, None
    conv_c = _conv_branch(z_ctx, wts['conv_w'], l, col_conv)
    pool_c = _pool_branch(z_ctx, wts['pool_w'], wts['pool_scale'], l, col_pool)
    y_ctx = _merge(u2_ctx, (s5_c, conv_c, pool_c, ml_c), wts['w_in_t'], wts['gate_row0'], wts['w_branch'], l)
    return y_lat, y_ctx


def _out_proj_resid(h, y, gate, w_out, l):
    b, t, d = h.shape
    m = b * t
    tm = min(m, ROW_TILE)
    tn = 512
    out = _mm(y, tm, tn, d, [_layer_w_spec(l, d, tn)], [w_out],
              [pl.BlockSpec((tm, tn), lambda i, j: (i, j)), _gate_spec(gate, tm, t, tn)],
              [h.reshape(m, d), gate], functools.partial(_resid_epilogue, coef=1.0), F32, "out_proj_resid")
    return out.reshape(b, t, d)


def kernel(x, c, ctx, c_ctx, w_ada, b_ada, w_ffn1_in, w_ffn1_out, w_ffn2_in, w_ffn2_out, w_in, s5_a_re, s5_a_im, s5_log_dt, s5_b_re, s5_b_im, s5_c_re, s5_c_im, s5_d, s5_w_glu, s5_b_glu, conv_w, pool_w, pool_scale, mlstm_gate_bias, mlstm_norm_gain, w_branch, w_out, final_gain):
    n_batch, t_lat, d = x.shape
    depth = w_ada.shape[0]
    w = d // 4
    n_main = 8 * w
    n_mgate = 4 * MLSTM_HEADS
    dk = w // MLSTM_HEADS // 2

    row_scale = jnp.ones((w_in.shape[2], 1), F32).at[5 * w:5 * w + w // 2].set(dk ** -0.5)
    w_in_t = (jnp.transpose(w_in, (0, 2, 1)) * row_scale).astype(BF16)
    wts = {
        'w_in_t': w_in_t,
        'gate_row0': n_main + n_mgate,
        'w_mgate_t': jnp.pad(w_in_t[:, n_main:n_main + n_mgate], ((0, 0), (0, LANES - n_mgate), (0, 0))),
        'mgate_bias': jnp.pad(mlstm_gate_bias.reshape(depth, 1, n_mgate).astype(F32),
                              ((0, 0), (0, 0), (0, LANES - n_mgate))),
        'w_branch': w_branch.astype(BF16),
        's5_w_glu': s5_w_glu.astype(BF16),
        's5_b_glu': s5_b_glu.astype(F32).reshape(depth, 1, w),
        'conv_w': conv_w.astype(F32),
        'pool_w': pool_w.astype(BF16),
        'pool_scale': pool_scale.astype(F32).reshape(depth, 1, w),
        'mlstm_norm_gain': mlstm_norm_gain.astype(F32).reshape(depth, 1, w),
        's5_tables': _s5_all_tables(s5_a_re, s5_a_im, s5_log_dt, s5_b_re, s5_b_im, s5_c_re, s5_c_im, s5_d),
    }
    w1i, w1o = w_ffn1_in.astype(BF16), w_ffn1_out.astype(BF16)
    w2i, w2o = w_ffn2_in.astype(BF16), w_ffn2_out.astype(BF16)
    w_o = w_out.astype(BF16)

    c_all = jnp.zeros((8, d), F32).at[:n_batch].set(c).at[n_batch].set(c_ctx)
    mods = _ada_mod(c_all, w_ada, b_ada).reshape(depth, 8, N_MOD, d)

    h, hc = x, ctx
    for l in range(depth):
        last = l == depth - 1
        mod = [mods[l, :n_batch, k][:, None, :] for k in range(N_MOD)]
        modc = [mods[l, n_batch:n_batch + 1, k][:, None, :] for k in range(N_MOD)]
        h = _ffn(h, mod[0], mod[1], mod[2], w1i, w1o, l)
        hc = _ffn(hc, modc[0], modc[1], modc[2], w1i, w1o, l)
        col_major = (l % 2) == 1
        u = _prep(h, mod[3], mod[4])
        uc = _prep(hc, modc[3], modc[4])
        if col_major:
            u = u.reshape(n_batch, t_lat // GRID_W, GRID_W, d).transpose(0, 2, 1, 3).reshape(n_batch, t_lat, d)
        y, yc = _token_mixer(u, uc, not last, wts, l)
        if col_major:
            y = y.reshape(n_batch, GRID_W, t_lat // GRID_W, d).transpose(0, 2, 1, 3).reshape(n_batch * t_lat, d)
        h = _out_proj_resid(h, y, mod[5], w_o, l)
        h = _ffn(h, mod[6], mod[7], mod[8], w2i, w2o, l)
        if not last:
            hc = _out_proj_resid(hc, yc, modc[5], w_o, l)
            hc = _ffn(hc, modc[6], modc[7], modc[8], w2i, w2o, l)
    return _final_norm(h, final_gain)
```

```python
import functools
import math

import jax
import jax.numpy as jnp
from jax import lax
from jax.experimental import pallas as pl
from jax.experimental.pallas import tpu as pltpu

F32 = jnp.float32
BF16 = jnp.bfloat16
EPS = 1e-6
GRID_W = 64
N_MOD = 9
N_BRANCH = 4
S5_P = 16
S5_N = 64
S5_CHUNK = 16
LANES = 128
S5_LANE_GROUPS = LANES // S5_P
POOL_WINDOWS = (2, 4, 8, 16)
MLSTM_HEADS = 4
MLSTM_CHUNK = 128
BF16_ROWS = 16
HALO = BF16_ROWS
ROW_TILE = 1024
VMEM_LIMIT = 56 * 1024 * 1024
HI = lax.Precision.HIGHEST


def _params(n_grid_dims):
    return pltpu.CompilerParams(dimension_semantics=("arbitrary",) * n_grid_dims,
                                vmem_limit_bytes=VMEM_LIMIT)


def _gelu_tanh(y):
    return 0.5 * y * (1.0 + jnp.tanh(math.sqrt(2.0 / math.pi) * (y + 0.044715 * (y * y * y))))


def _log_sigmoid(x):
    return jnp.minimum(x, 0.0) - jnp.log1p(jnp.exp(-jnp.abs(x)))


def _ada_kernel(c_ref, w_ref, b_ref, o_ref):
    c = c_ref[...]
    a = (c * jax.nn.sigmoid(c)).astype(BF16)
    o_ref[0] = jnp.dot(a, w_ref[0].astype(BF16), preferred_element_type=F32) + b_ref[0]


def _ada_mod(c_all, w_ada, b_ada):
    n_layers, d, nd = w_ada.shape
    tn = 1024
    return pl.pallas_call(
        _ada_kernel,
        grid=(n_layers, nd // tn),
        in_specs=[pl.BlockSpec((8, d), lambda l, j: (0, 0)),
                  pl.BlockSpec((1, d, tn), lambda l, j: (l, 0, j)),
                  pl.BlockSpec((1, 1, tn), lambda l, j: (l, 0, j))],
        out_specs=pl.BlockSpec((1, 8, tn), lambda l, j: (l, 0, j)),
        out_shape=jax.ShapeDtypeStruct((n_layers, 8, nd), F32),
        compiler_params=_params(2), name="ada_mod",
    )(c_all, w_ada, b_ada.reshape(n_layers, 1, nd))


def _prep_kernel(h_ref, shift_ref, scale_ref, o_ref):
    x = h_ref[0]
    ms = jnp.mean(x * x, axis=-1, keepdims=True)
    o_ref[0] = (x * lax.rsqrt(ms + EPS) * (1.0 + scale_ref[0]) + shift_ref[0]).astype(o_ref.dtype)


def _prep(h, shift, scale):
    b, t, d = h.shape
    per_batch = shift.shape[0] > 1
    mod_spec = pl.BlockSpec((1, 1, d), lambda bi, i: (bi if per_batch else 0, 0, 0))
    tr = min(t, 512)
    spec = pl.BlockSpec((1, tr, d), lambda bi, i: (bi, i, 0))
    return pl.pallas_call(
        _prep_kernel, grid=(b, t // tr), in_specs=[spec, mod_spec, mod_spec], out_specs=spec,
        out_shape=jax.ShapeDtypeStruct((b, t, d), BF16),
        compiler_params=_params(2), name="prep",
    )(h, shift, scale)


def _final_kernel(h_ref, g_ref, o_ref):
    x = h_ref[0]
    ms = jnp.mean(x * x, axis=-1, keepdims=True)
    o_ref[0] = x * lax.rsqrt(ms + EPS) * g_ref[...]


def _final_norm(h, gain):
    b, t, d = h.shape
    tr = min(t, 512)
    spec = pl.BlockSpec((1, tr, d), lambda bi, i: (bi, i, 0))
    return pl.pallas_call(
        _final_kernel, grid=(b, t // tr),
        in_specs=[spec, pl.BlockSpec((1, d), lambda bi, i: (0, 0))], out_specs=spec,
        out_shape=jax.ShapeDtypeStruct((b, t, d), F32),
        compiler_params=_params(2), name="final_norm",
    )(h, gain.reshape(1, d))


_NT_DIMS = (((1,), (1,)), ((), ()))


def _mm_kernel(*refs, n_w, n_extra, epilogue, w_transposed):
    x_ref = refs[0]
    w_refs = refs[1:1 + n_w]
    extra = refs[1 + n_w:1 + n_w + n_extra]
    o_ref = refs[1 + n_w + n_extra]
    x = x_ref[...]
    if w_transposed:
        accs = [lax.dot_general(x, w[...], _NT_DIMS, preferred_element_type=F32) for w in w_refs]
    else:
        accs = [jnp.dot(x, w[...], preferred_element_type=F32) for w in w_refs]
    o_ref[...] = epilogue(accs, x, extra).astype(o_ref.dtype)


def _mm(x, tm, tn, n_cols, w_specs, w_arrays, extra_specs, extra_arrays, epilogue, out_dtype, name,
        w_transposed=False):
    m, k = x.shape
    return pl.pallas_call(
        functools.partial(_mm_kernel, n_w=len(w_arrays), n_extra=len(extra_arrays), epilogue=epilogue,
                          w_transposed=w_transposed),
        grid=(m // tm, n_cols // tn),
        in_specs=[pl.BlockSpec((tm, k), lambda i, j: (i, 0))] + list(w_specs) + list(extra_specs),
        out_specs=pl.BlockSpec((tm, tn), lambda i, j: (i, j)),
        out_shape=jax.ShapeDtypeStruct((m, n_cols), out_dtype),
        compiler_params=_params(2), name=name,
    )(x, *w_arrays, *extra_arrays)


def _layer_w_spec(l, k, tn, col_block_offset=0):
    return pl.BlockSpec((None, k, tn), lambda i, j: (l, 0, col_block_offset + j))


def _layer_wt_spec(l, k, tn, row_block_offset=0):
    return pl.BlockSpec((None, tn, k), lambda i, j: (l, row_block_offset + j, 0))


def _gate_spec(gate, tm, rows_per_batch, tn):
    per_batch = gate.shape[0] > 1
    return pl.BlockSpec((1, 1, tn), lambda i, j: ((i * tm) // rows_per_batch if per_batch else 0, 0, j))


def _swiglu_epilogue(accs, x, extra):
    a, g = accs
    return g * jax.nn.sigmoid(g) * a


def _resid_epilogue(accs, x, extra, *, coef):
    h_ref, gate_ref = extra
    return h_ref[...] + (coef * gate_ref[0]) * accs[0]


def _plain_epilogue(accs, x, extra):
    return accs[0]


def _bias_epilogue(accs, x, extra):
    return accs[0] + extra[0][...]


def _glu_epilogue(accs, x, extra):
    return x.astype(F32) * jax.nn.sigmoid(accs[0] + extra[0][...])


def _ffn(h, shift, scale, gate, w_in, w_out, l):
    b, t, d = h.shape
    m = b * t
    d_ff = w_out.shape[1]
    tm = min(m, ROW_TILE)
    tn = 512
    xn = _prep(h, shift, scale).reshape(m, d)
    act = _mm(xn, tm, tn, d_ff,
              [_layer_w_spec(l, d, tn), _layer_w_spec(l, d, tn, d_ff // tn)], [w_in, w_in],
              [], [], _swiglu_epilogue, BF16, "ffn_in")
    out = _mm(act, tm, tn, d,
              [_layer_w_spec(l, d_ff, tn)], [w_out],
              [pl.BlockSpec((tm, tn), lambda i, j: (i, j)), _gate_spec(gate, tm, t, tn)],
              [h.reshape(m, d), gate],
              functools.partial(_resid_epilogue, coef=0.5), F32, "ffn_out")
    return out.reshape(b, t, d)


def _merge_kernel(u_ref, b0, b1, b2, b3, wg_ref, wb_ref, o_ref, acc_ref):
    g = pl.program_id(2)
    gate = jax.nn.sigmoid(lax.dot_general(u_ref[...], wg_ref[0], _NT_DIMS, preferred_element_type=F32))
    for k, br in enumerate((b0, b1, b2, b3)):
        @pl.when(g == k)
        def _(k=k, br=br):
            val = gate * jnp.dot(br[...], wb_ref[...], preferred_element_type=F32)
            if k == 0:
                acc_ref[...] = val
            else:
                acc_ref[...] += val

    @pl.when(g == N_BRANCH - 1)
    def _():
        o_ref[...] = acc_ref[...].astype(o_ref.dtype)


def _merge(u, branches, w_in_t, gate_row0, w_branch, l):
    m, d = u.shape
    w = branches[0].shape[1]
    tm = min(m, ROW_TILE)
    tn = 512
    nj = d // tn
    assert gate_row0 % BF16_ROWS == 0, "gate rows must start on a packed sublane tile"
    br_spec = pl.BlockSpec((tm, w), lambda i, j, g: (i, 0))
    return pl.pallas_call(
        _merge_kernel, grid=(m // tm, nj, N_BRANCH),
        in_specs=[pl.BlockSpec((tm, d), lambda i, j, g: (i, 0)), br_spec, br_spec, br_spec, br_spec,
                  pl.BlockSpec((pl.Element(1), pl.Element(tn), pl.Element(d)),
                               lambda i, j, g: (l, pl.multiple_of(gate_row0 + (g * nj + j) * tn, BF16_ROWS), 0)),
                  pl.BlockSpec((None, None, w, tn), lambda i, j, g: (l, g, 0, j))],
        out_specs=pl.BlockSpec((tm, tn), lambda i, j, g: (i, j)),
        out_shape=jax.ShapeDtypeStruct((m, d), BF16),
        scratch_shapes=[pltpu.VMEM((tm, tn), F32)],
        compiler_params=_params(3), name="merge",
    )(u, *branches, w_in_t, w_branch)


def _s5_factors(a_re, a_im, log_dt, b_re, b_im, c_re, c_im):
    a_re, a_im = a_re.astype(F32), a_im.astype(F32)
    n_groups = a_re.shape[1]
    lc, p, n, gl = S5_CHUNK, S5_P, S5_N, S5_LANE_GROUPS
    nb = n_groups // gl
    dt = jnp.exp(log_dt.astype(F32))[:, :, None]
    ks = jnp.arange(lc + 1, dtype=F32)[:, None, None, None]
    mag = jnp.exp(ks * (dt * a_re))
    pw = jnp.stack([mag * jnp.cos(ks * (dt * a_im)), mag * jnp.sin(ks * (dt * a_im))])
    lam_re, lam_im = pw[0, 1], pw[1, 1]
    den = a_re * a_re + a_im * a_im
    z_re = ((lam_re - 1.0) * a_re + lam_im * a_im) / den
    z_im = (lam_im * a_re - (lam_re - 1.0) * a_im) / den
    b_re, b_im = b_re.astype(F32), b_im.astype(F32)
    bb = jnp.stack([z_re[..., None] * b_re - z_im[..., None] * b_im,
                    z_re[..., None] * b_im + z_im[..., None] * b_re])
    cc = jnp.stack([c_re.astype(F32), c_im.astype(F32)])
    same = jnp.eye(gl, dtype=F32)
    cb = cc.reshape(2, 2, nb, gl, p, n).transpose(2, 1, 0, 3, 5, 4)
    cb = (cb[:, :, :, :, :, None, :] * same[None, None, None, :, None, :, None]).reshape(nb, 2, 2, gl * n, gl * p)
    bd = bb.reshape(2, 2, nb, gl, n, p).transpose(2, 1, 0, 3, 5, 4)
    bd = (bd[:, :, :, :, :, None, :] * same[None, None, None, :, None, :, None]).reshape(nb, 2, 2, gl * p, gl * n)
    pwr = pw.reshape(2, lc + 1, 2, nb, gl * n).transpose(3, 2, 0, 1, 4)
    pwc = pwr.transpose(0, 1, 2, 4, 3)
    return cb, bd, pwr, pwc


def _s5_tables_kernel(cb_ref, bd_ref, pwr_ref, pwc_ref, skip_ref, toep_ref, ec_ref, bc_ref, k_ref):
    s = pl.program_id(1)
    lc = S5_CHUNK
    ncol = pwc_ref.shape[3]

    def column(d, comp, e):
        lane = lax.broadcasted_iota(jnp.int32, (ncol, pwc_ref.shape[4]), 1)
        return jnp.sum(jnp.where(lane == e, pwc_ref[0, d, comp], 0.0), axis=1, keepdims=True)

    def c_times_power(d, e):
        pr, pi = column(d, 0, e), column(d, 1, e)
        return cb_ref[0, d, 0] * pr - cb_ref[0, d, 1] * pi, cb_ref[0, d, 0] * pi + cb_ref[0, d, 1] * pr

    @pl.when(s == 0)
    def _():
        for d in range(2):
            x_cat = jnp.concatenate([bd_ref[0, d, 0], -bd_ref[0, d, 1]], axis=1)
            for tau in range(lc):
                yr, yi = c_times_power(d, tau)
                k_ref[d, tau] = jnp.dot(x_cat, jnp.concatenate([yr, yi], axis=0), precision=HI,
                                        preferred_element_type=F32)

    rows = lax.broadcasted_iota(jnp.int32, (LANES, LANES), 0)
    cols = lax.broadcasted_iota(jnp.int32, (LANES, LANES), 1)
    diag = jnp.where(rows == cols, skip_ref[0], 0.0)
    for r in range(lc):
        blk = (jnp.where(s >= r, k_ref[0, jnp.maximum(s - r, 0)], 0.0)
               + jnp.where(s <= r, k_ref[1, jnp.maximum(r - s, 0)], 0.0)
               + jnp.where(s == r, diag, 0.0))
        toep_ref[0, r * LANES:(r + 1) * LANES, :] = blk.astype(toep_ref.dtype)
    for d in range(2):
        yr, yi = c_times_power(d, (s + 1) if d == 0 else (lc - s))
        ec_ref[0, d * ncol:(d + 1) * ncol, :] = yr.astype(ec_ref.dtype)
        ec_ref[0, (2 + d) * ncol:(3 + d) * ncol, :] = (-yi).astype(ec_ref.dtype)
        f = (lc - 1 - s) if d == 0 else s
        fr, fi = pwr_ref[0, d, 0, pl.ds(f, 1), :], pwr_ref[0, d, 1, pl.ds(f, 1), :]
        bc_ref[0, :, d * ncol:(d + 1) * ncol] = (bd_ref[0, d, 0] * fr - bd_ref[0, d, 1] * fi).astype(bc_ref.dtype)
        bc_ref[0, :, (2 + d) * ncol:(3 + d) * ncol] = (bd_ref[0, d, 0] * fi + bd_ref[0, d, 1] * fr).astype(bc_ref.dtype)


def _s5_all_tables(a_re, a_im, log_dt, b_re, b_im, c_re, c_im, d_skip):
    cb, bd, pwr, pwc = jax.vmap(_s5_factors)(a_re, a_im, log_dt, b_re, b_im, c_re, c_im)
    depth, nb = cb.shape[:2]
    lc = S5_CHUNK
    nk = depth * nb
    flat = lambda t: t.reshape((nk,) + t.shape[2:])
    cb, bd, pwr, pwc = flat(cb), flat(bd), flat(pwr), flat(pwc)
    ncol, nrow = cb.shape[3], cb.shape[4]
    skip = d_skip.astype(F32).reshape(nk, 1, nrow)
    whole = lambda t: pl.BlockSpec((1,) + t.shape[1:], lambda k, s: (k,) + (0,) * (t.ndim - 1))
    toep, ec, bc = pl.pallas_call(
        _s5_tables_kernel, grid=(nk, lc),
        in_specs=[whole(cb), whole(bd), whole(pwr), whole(pwc), whole(skip)],
        out_specs=[pl.BlockSpec((1, lc * nrow, nrow), lambda k, s: (k, 0, s)),
                   pl.BlockSpec((1, 4 * ncol, nrow), lambda k, s: (k, 0, s)),
                   pl.BlockSpec((1, nrow, 4 * ncol), lambda k, s: (k, s, 0))],
        out_shape=[jax.ShapeDtypeStruct((nk, lc * nrow, lc * nrow), BF16),
                   jax.ShapeDtypeStruct((nk, 4 * ncol, lc * nrow), BF16),
                   jax.ShapeDtypeStruct((nk, lc * nrow, 4 * ncol), BF16)],
        scratch_shapes=[pltpu.VMEM((2, lc, nrow, nrow), F32)],
        compiler_params=_params(2), name="s5_tables",
    )(cb, bd, pwr, pwc, skip)
    unflat = lambda t: t.reshape((depth, nb) + t.shape[1:])
    groups_states = nb * ncol
    lam_re = pwr[:, :, 0, lc].reshape(depth, nb, 2, ncol).transpose(0, 2, 1, 3).reshape(depth, 2, 1, groups_states)
    lam_im = pwr[:, :, 1, lc].reshape(depth, nb, 2, ncol).transpose(0, 2, 1, 3).reshape(depth, 2, 1, groups_states)
    return unflat(toep), unflat(ec), unflat(bc), lam_re, lam_im


def _s5_proj_kernel(x_ref, w_ref, o_ref, acc_ref):
    acc = lax.dot_general(x_ref[...], w_ref[...], _NT_DIMS, preferred_element_type=F32)
    n_rows = o_ref.shape[1]
    for blk in range(o_ref.shape[0]):
        acc_ref[blk] = acc[:, blk * LANES:(blk + 1) * LANES]
        for r in range(S5_CHUNK):
            rows = acc_ref[blk, pl.ds(r, n_rows, stride=S5_CHUNK), :]
            o_ref[blk, :, r * LANES:(r + 1) * LANES] = rows.astype(o_ref.dtype)


def _s5_proj(u2, w_main_t, l, w):
    m, d = u2.shape
    tm = min(m, ROW_TILE)
    nb = w // LANES
    return pl.pallas_call(
        _s5_proj_kernel, grid=(m // tm,),
        in_specs=[pl.BlockSpec((tm, d), lambda i: (i, 0)), pl.BlockSpec((None, w, d), lambda i: (l, 0, 0))],
        out_specs=pl.BlockSpec((nb, tm // S5_CHUNK, S5_CHUNK * LANES), lambda i: (0, i, 0)),
        out_shape=jax.ShapeDtypeStruct((nb, m // S5_CHUNK, S5_CHUNK * LANES), BF16),
        scratch_shapes=[pltpu.VMEM((nb, tm, LANES), F32)],
        compiler_params=_params(1), name="s5_proj",
    )(u2, w_main_t)


def _s5_in_kernel(u_ref, w_ref, re_ref, im_ref):
    acc = jnp.dot(u_ref[0], w_ref[0], preferred_element_type=F32)
    w = re_ref.shape[2]
    re_ref[0] = acc[:, 0:w]
    re_ref[1] = acc[:, w:2 * w]
    im_ref[0] = acc[:, 2 * w:3 * w]
    im_ref[1] = acc[:, 3 * w:4 * w]


def _s5_scan_kernel(lre, lim, cre, cim, lr_ref, li_ref, o_lre, o_lim, o_cre, o_cim, *, n_batch, n_lat, n_ctx):
    d = pl.program_id(0)
    lr = lr_ref[0]
    li = li_ref[0]

    def run(re_ref, im_ref, ore_ref, oim_ref, n_chunks, carry):
        def body(j, carry):
            idx = jnp.where(d == 0, j, n_chunks - 1 - j)
            new = []
            for bi in range(n_batch):
                xr, xi = carry[bi]
                row = bi * n_chunks + idx
                ore_ref[0, pl.ds(row, 1), :] = xr
                oim_ref[0, pl.ds(row, 1), :] = xi
                ur = re_ref[0, pl.ds(row, 1), :]
                ui = im_ref[0, pl.ds(row, 1), :]
                new.append((lr * xr - li * xi + ur, lr * xi + li * xr + ui))
            return tuple(new)

        return lax.fori_loop(0, n_chunks, body, carry)

    zero = jnp.zeros(lr.shape, F32)
    carry = run(cre, cim, o_cre, o_cim, n_ctx, tuple((zero, zero) for _ in range(n_batch)))
    run(lre, lim, o_lre, o_lim, n_lat, carry)


def _s5_out_kernel(u_ref, xr_ref, xi_ref, t_ref, e_ref, o_ref):
    xcat = jnp.concatenate([xr_ref[0], xr_ref[1], xi_ref[0], xi_ref[1]], axis=1).astype(BF16)
    y = (jnp.dot(u_ref[0], t_ref[0], preferred_element_type=F32)
         + jnp.dot(xcat, e_ref[0], preferred_element_type=F32))
    o_ref[0] = _gelu_tanh(y).astype(o_ref.dtype)


def _s5_glu_kernel(y_ref, w_ref, b_ref, o_ref, g_ref):
    n_rows = y_ref.shape[1]
    for blk in range(y_ref.shape[0]):
        for s in range(S5_CHUNK):
            g_ref[blk, pl.ds(s, n_rows, stride=S5_CHUNK), :] = y_ref[blk, :, s * LANES:(s + 1) * LANES].astype(F32)
    g = jnp.concatenate([g_ref[blk] for blk in range(y_ref.shape[0])], axis=1)
    z = jnp.dot(g.astype(BF16), w_ref[...], preferred_element_type=F32) + b_ref[...]
    o_ref[...] = (g * jax.nn.sigmoid(z)).astype(o_ref.dtype)


def _s5_branch(uc_lat, uc_ctx, n_batch, tables, w_glu, b_glu, l, want_ctx):
    toep, ec, bc, lam_re, lam_im = tables
    nb, r_lat, cw = uc_lat.shape
    r_ctx = uc_ctx.shape[1]
    gn = lam_re.shape[3]
    sw = gn // nb
    w = nb * LANES

    def chunk_rows(r):
        return min(r, 512)

    def state_in(uc):
        r = uc.shape[1]
        rt = chunk_rows(r)
        x_spec = pl.BlockSpec((2, rt, sw), lambda k, i: (0, i, k))
        x_shape = jax.ShapeDtypeStruct((2, r, gn), F32)
        return pl.pallas_call(
            _s5_in_kernel, grid=(nb, r // rt),
            in_specs=[pl.BlockSpec((1, rt, cw), lambda k, i: (k, i, 0)),
                      pl.BlockSpec((None, 1, cw, 4 * sw), lambda k, i: (l, k, 0, 0))],
            out_specs=[x_spec, x_spec], out_shape=[x_shape, x_shape],
            compiler_params=_params(2), name="s5_in",
        )(uc, bc)

    lre, lim = state_in(uc_lat)
    cre, cim = state_in(uc_ctx)
    lb = 512
    lat_blk = pl.BlockSpec((1, r_lat, lb), lambda d, j: (d, 0, j))
    ctx_blk = pl.BlockSpec((1, r_ctx, lb), lambda d, j: (d, 0, j))
    lam_spec = pl.BlockSpec((None, 1, 1, lb), lambda d, j: (l, d, 0, j))
    lat_shape = jax.ShapeDtypeStruct((2, r_lat, gn), F32)
    ctx_shape = jax.ShapeDtypeStruct((2, r_ctx, gn), F32)
    xl_re, xl_im, xc_re, xc_im = pl.pallas_call(
        functools.partial(_s5_scan_kernel, n_batch=n_batch, n_lat=r_lat // n_batch, n_ctx=r_ctx // n_batch),
        grid=(2, gn // lb),
        in_specs=[lat_blk, lat_blk, ctx_blk, ctx_blk, lam_spec, lam_spec],
        out_specs=[lat_blk, lat_blk, ctx_blk, ctx_blk],
        out_shape=[lat_shape, lat_shape, ctx_shape, ctx_shape],
        compiler_params=_params(2), name="s5_scan",
    )(lre, lim, cre, cim, lam_re, lam_im)

    def readout(uc, x_re, x_im):
        r = uc.shape[1]
        rt = chunk_rows(r)
        hw = cw // 2
        x_spec = pl.BlockSpec((2, rt, sw), lambda k, h, i: (0, i, k))
        return pl.pallas_call(
            _s5_out_kernel, grid=(nb, 2, r // rt),
            in_specs=[pl.BlockSpec((1, rt, cw), lambda k, h, i: (k, i, 0)), x_spec, x_spec,
                      pl.BlockSpec((None, 1, cw, hw), lambda k, h, i: (l, k, 0, h)),
                      pl.BlockSpec((None, 1, 4 * sw, hw), lambda k, h, i: (l, k, 0, h))],
            out_specs=pl.BlockSpec((1, rt, hw), lambda k, h, i: (k, i, h)),
            out_shape=jax.ShapeDtypeStruct((nb, r, cw), BF16),
            compiler_params=_params(3), name="s5_out",
        )(uc, x_re, x_im, toep, ec)

    def glu(y):
        m = y.shape[1] * S5_CHUNK
        tm = min(m, ROW_TILE)
        return pl.pallas_call(
            _s5_glu_kernel, grid=(m // tm,),
            in_specs=[pl.BlockSpec((nb, tm // S5_CHUNK, cw), lambda i: (0, i, 0)),
                      pl.BlockSpec((None, w, w), lambda i: (l, 0, 0)),
                      pl.BlockSpec((None, 1, w), lambda i: (l, 0, 0))],
            out_specs=pl.BlockSpec((tm, w), lambda i: (i, 0)),
            out_shape=jax.ShapeDtypeStruct((m, w), BF16),
            scratch_shapes=[pltpu.VMEM((nb, tm, LANES), F32)],
            compiler_params=_params(1), name="s5_glu",
        )(y, w_glu, b_glu)

    out_lat = glu(readout(uc_lat, xl_re, xl_im))
    out_ctx = glu(readout(uc_ctx, xc_re, xc_im)) if want_ctx else None
    return out_lat, out_ctx


def _conv_kernel(hm, hp, hn, cm, cp, cn, bm, w_ref, o_ref, *, n_tiles):
    i = pl.program_id(1)
    xg = hm[0].astype(F32) * cm[0].astype(F32)
    tm = xg.shape[0]
    x_before = jnp.where(i > 0, hp[0, HALO - 1:HALO].astype(F32) * cp[0, HALO - 1:HALO].astype(F32), 0.0)
    x_after = jnp.where(i < n_tiles - 1, hn[0, 0:1].astype(F32) * cn[0, 0:1].astype(F32), 0.0)
    row = lax.broadcasted_iota(jnp.int32, xg.shape, 0)
    prev = jnp.where(row == 0, x_before, pltpu.roll(xg, 1, axis=0))
    nxt = jnp.where(row == tm - 1, x_after, pltpu.roll(xg, tm - 1, axis=0))
    w = w_ref[...]
    y = w[0:1] * prev + w[1:2] * xg + w[2:3] * nxt
    o_ref[0] = (bm[0].astype(F32) * y).astype(o_ref.dtype)


def _halo_specs(tm, width, col_block, n_halo_blocks):
    hb = tm // HALO
    main = pl.BlockSpec((1, tm, width), lambda b, i, c: (b, i, col_block + c))
    before = pl.BlockSpec((1, HALO, width), lambda b, i, c: (b, jnp.maximum(i * hb - 1, 0), col_block + c))
    after = pl.BlockSpec((1, HALO, width),
                         lambda b, i, c: (b, jnp.minimum((i + 1) * hb, n_halo_blocks - 1), col_block + c))
    return main, before, after


def _conv_branch(z, conv_w, l, col0):
    b, t, _ = z.shape
    w = conv_w.shape[2]
    tm = min(t, ROW_TILE)
    tc = 512
    n_tiles = t // tm
    cb = col0 // tc
    h_specs = _halo_specs(tm, tc, cb, t // HALO)
    bg_spec = _halo_specs(tm, tc, cb + w // tc, t // HALO)[0]
    c_specs = _halo_specs(tm, tc, cb + 2 * (w // tc), t // HALO)
    out = pl.pallas_call(
        functools.partial(_conv_kernel, n_tiles=n_tiles), grid=(b, n_tiles, w // tc),
        in_specs=[*h_specs, *c_specs, bg_spec, pl.BlockSpec((None, 3, tc), lambda bi, i, c: (l, 0, c))],
        out_specs=pl.BlockSpec((1, tm, tc), lambda bi, i, c: (bi, i, c)),
        out_shape=jax.ShapeDtypeStruct((b, t, w), BF16),
        compiler_params=_params(3), name="conv",
    )(z, z, z, z, z, z, z, conv_w)
    return out.reshape(b * t, w)


def _pool_kernel(um, up, un, w_ref, s_ref, o_ref, *, n_tiles, seq_len):
    i = pl.program_id(1)
    gi = pl.program_id(2)
    xm = um[0]
    tm = xm.shape[0]
    win = jnp.left_shift(POOL_WINDOWS[0], gi)
    half = win // 2
    x_before = jnp.where(i > 0, up[0], jnp.zeros_like(up[0]))
    x_after = jnp.where(i < n_tiles - 1, un[0], jnp.zeros_like(un[0]))
    ext = jnp.concatenate([x_before, xm, x_after], axis=0)
    s = lax.broadcasted_iota(jnp.int32, (tm, tm + 2 * HALO), 0)
    r = lax.broadcasted_iota(jnp.int32, (tm, tm + 2 * HALO), 1) - HALO
    off = r - s
    band = jnp.where(off >= -half, jnp.where(off < win - half, 1.0, 0.0), 0.0).astype(BF16)
    wsum = jnp.dot(band, ext, preferred_element_type=F32)
    t = i * tm + lax.broadcasted_iota(jnp.int32, (tm, 1), 0)
    cnt = (jnp.minimum(t + win - half, seq_len) - jnp.maximum(t - half, 0)).astype(F32)
    p = wsum / cnt - xm.astype(F32)
    y = jnp.dot(p.astype(BF16), w_ref[...], preferred_element_type=F32) * s_ref[...]
    o_ref[0] = y.astype(o_ref.dtype)


def _pool_branch(z, pool_w, pool_scale, l, col0):
    b, t, _ = z.shape
    n_groups, gw = pool_w.shape[1], pool_w.shape[2]
    tm = min(t, 512)
    n_tiles = t // tm
    specs = _halo_specs(tm, gw, col0 // gw, t // HALO)
    out = pl.pallas_call(
        functools.partial(_pool_kernel, n_tiles=n_tiles, seq_len=t), grid=(b, n_tiles, n_groups),
        in_specs=[*specs, pl.BlockSpec((None, None, gw, gw), lambda bi, i, c: (l, c, 0, 0)),
                  pl.BlockSpec((None, 1, gw), lambda bi, i, c: (l, 0, c))],
        out_specs=pl.BlockSpec((1, tm, gw), lambda bi, i, c: (bi, i, c)),
        out_shape=jax.ShapeDtypeStruct((b, t, n_groups * gw), BF16),
        compiler_params=_params(3), name="pool",
    )(z, z, z, pool_w, pool_scale)
    return out.reshape(b * t, n_groups * gw)


def _mlstm_chunk(q, k, v, li_col, lf_col, li_row, lf_row, state, reverse):
    c_mat, n_row, m = state
    lc = q.shape[0]
    ti = lax.broadcasted_iota(jnp.int32, (lc, lc), 0)
    si = lax.broadcasted_iota(jnp.int32, (lc, lc), 1)
    seen = (si >= ti) if reverse else (si <= ti)
    seen_t = (ti >= si) if reverse else (ti <= si)
    bcum_col = jnp.sum(jnp.where(seen, lf_row, 0.0), axis=1, keepdims=True)
    bcum_row = jnp.sum(jnp.where(seen_t, lf_col, 0.0), axis=0, keepdims=True)
    b_last = jnp.sum(lf_row, axis=1, keepdims=True)
    dmat = bcum_col - bcum_row + li_row
    m_inter = bcum_col + m
    m_t = jnp.maximum(jnp.max(jnp.where(seen, dmat, -1e30), axis=1, keepdims=True), m_inter)
    scores = lax.dot_general(q, k, (((1,), (1,)), ((), ())), preferred_element_type=F32)
    wgt = jnp.where(seen, scores * jnp.exp(dmat - m_t), 0.0)
    decay = jnp.exp(m_inter - m_t)
    num = (jnp.dot(wgt.astype(BF16), v, preferred_element_type=F32)
           + decay * jnp.dot(q, c_mat.astype(BF16), preferred_element_type=F32))
    den = (jnp.sum(wgt, axis=1, keepdims=True)
           + decay * jnp.sum(q.astype(F32) * n_row, axis=1, keepdims=True))
    h = num / jnp.maximum(jnp.abs(den), jnp.exp(-m_t))
    g_col = b_last - bcum_col + li_col
    m_new = jnp.maximum(b_last + m, jnp.max(g_col, axis=0, keepdims=True))
    carry = jnp.exp(b_last + m - m_new)
    wk = k.astype(F32) * jnp.exp(g_col - m_new)
    c_new = carry * c_mat + jnp.dot(wk.T.astype(BF16), v, preferred_element_type=F32)
    n_new = carry * n_row + jnp.sum(wk, axis=0, keepdims=True)
    return h, (c_new, n_new, m_new)


def _mlstm_kernel(ql, kl, vl, ol, gcl, grl, qc, kc, vc, oc, gcc, grc, gain_ref, out_l, out_c, h_l, h_c):
    lc = MLSTM_CHUNK
    dk, dv = ql.shape[2], vl.shape[2]

    def run(refs, h_ref, n_chunks, states):
        q_ref, k_ref, v_ref, gc_ref, gr_ref = refs

        def one(c, st, d):
            t0 = pl.multiple_of(c * lc, lc)
            gcol = gc_ref[0, 0, pl.ds(t0, lc), :]
            li_col = gcol[:, d:d + 1]
            lf_col = _log_sigmoid(gcol[:, 2 + d:3 + d])
            li_row = gr_ref[0, 0, d, pl.ds(c, 1), :]
            lf_row = _log_sigmoid(gr_ref[0, 0, 2 + d, pl.ds(c, 1), :])
            h, st = _mlstm_chunk(q_ref[0, pl.ds(t0, lc), :], k_ref[0, pl.ds(t0, lc), :],
                                 v_ref[0, pl.ds(t0, lc), :], li_col, lf_col, li_row, lf_row, st, d == 1)
            return t0, h, st

        def make_body(second_visit):
            def body(j, sts):
                t_f, h_f, st_f = one(j, sts[0], 0)
                t_b, h_b, st_b = one(n_chunks - 1 - j, sts[1], 1)
                for t0, h in ((t_f, h_f), (t_b, h_b)):
                    if second_visit:
                        h_ref[pl.ds(t0, lc), :] += h
                    else:
                        h_ref[pl.ds(t0, lc), :] = h
                return (st_f, st_b)
            return body

        half = n_chunks // 2
        states = lax.fori_loop(0, half, make_body(False), states)
        return lax.fori_loop(half, n_chunks, make_body(True), states)

    lat = (ql, kl, vl, gcl, grl)
    ctx = (qc, kc, vc, gcc, grc)
    n_lat, n_ctx = ql.shape[1] // lc, qc.shape[1] // lc
    zero = (jnp.zeros((dk, dv), F32), jnp.zeros((1, dk), F32), jnp.zeros((1, 1), F32))
    states = run(ctx, h_c, n_ctx, (zero, zero))
    run(lat, h_l, n_lat, states)

    gain = gain_ref[...]

    def readout(h_ref, o_ref, out_ref):
        t = h_ref.shape[0]
        tile = min(t, 512)

        def body(i, _):
            t0 = pl.multiple_of(i * tile, tile)
            h = h_ref[pl.ds(t0, tile), :]
            hn = h * lax.rsqrt(jnp.mean(h * h, axis=-1, keepdims=True) + EPS) * gain
            og = jax.nn.sigmoid(o_ref[0, pl.ds(t0, tile), :].astype(F32))
            out_ref[0, pl.ds(t0, tile), :] = (hn * og).astype(out_ref.dtype)
            return 0

        lax.fori_loop(0, t // tile, body, 0)

    readout(h_l, ol, out_l)
    readout(h_c, oc, out_c)


def _mlstm_branch(z_lat, zg_lat, z_ctx, zg_ctx, norm_gain, l, col_q, col_k, col_v, col_o):
    b, t, _ = z_lat.shape
    tc = z_ctx.shape[1]
    nh = MLSTM_HEADS
    w = norm_gain.shape[2]
    dv = w // nh
    dk = dv // 2
    lc = MLSTM_CHUNK
    assert (t // lc) % 2 == 0 and (tc // lc) % 2 == 0, "the two-direction loop pairs chunks"

    def gate_layouts(zg, tt):
        g = zg[:, :4 * nh].reshape(b, tt, 4, nh)
        col = g.transpose(0, 3, 1, 2)
        row = g.transpose(0, 3, 2, 1).reshape(b, nh, 4, tt // lc, lc)
        return col, row

    gcl, grl = gate_layouts(zg_lat, t)
    gcc, grc = gate_layouts(zg_ctx, tc)
    once = pl.Buffered(1)

    def specs(tt):
        return [pl.BlockSpec((1, tt, dk), lambda bi, h: (bi, 0, col_q // dk + h), pipeline_mode=once),
                pl.BlockSpec((1, tt, dk), lambda bi, h: (bi, 0, col_k // dk + h), pipeline_mode=once),
                pl.BlockSpec((1, tt, dv), lambda bi, h: (bi, 0, col_v // dv + h), pipeline_mode=once),
                pl.BlockSpec((1, tt, dv), lambda bi, h: (bi, 0, col_o // dv + h), pipeline_mode=once),
                pl.BlockSpec((1, 1, tt, 4), lambda bi, h: (bi, h, 0, 0), pipeline_mode=once),
                pl.BlockSpec((1, 1, 4, tt // lc, lc), lambda bi, h: (bi, h, 0, 0, 0))]

    out_l, out_c = pl.pallas_call(
        _mlstm_kernel, grid=(b, nh),
        in_specs=specs(t) + specs(tc) + [pl.BlockSpec((None, 1, dv), lambda bi, h: (l, 0, h))],
        out_specs=[pl.BlockSpec((1, t, dv), lambda bi, h: (bi, 0, h)),
                   pl.BlockSpec((1, tc, dv), lambda bi, h: (bi, 0, h))],
        out_shape=[jax.ShapeDtypeStruct((b, t, w), BF16), jax.ShapeDtypeStruct((b, tc, w), BF16)],
        scratch_shapes=[pltpu.VMEM((t, dv), F32), pltpu.VMEM((tc, dv), F32)],
        compiler_params=_params(2), name="mlstm",
    )(z_lat, z_lat, z_lat, z_lat, gcl, grl, z_ctx, z_ctx, z_ctx, z_ctx, gcc, grc, norm_gain)
    return out_l.reshape(b * t, w), out_c.reshape(b * tc, w)


def _token_mixer(u_lat, u_ctx, want_ctx, wts, l):
    b, t, d = u_lat.shape
    tc = u_ctx.shape[1]
    w = d // 4
    n_rest = 7 * w
    col_conv, col_pool, col_q, col_k, col_v, col_o = 0, 3 * w, 4 * w, 4 * w + w // 2, 5 * w, 6 * w

    def in_proj(u):
        m = u.shape[0] * u.shape[1]
        u2 = u.reshape(m, d)
        tm = min(m, ROW_TILE)
        us5 = _s5_proj(u2, wts['w_in_t'], l, w)
        z = _mm(u2, tm, w, n_rest, [_layer_wt_spec(l, d, w, 1)], [wts['w_in_t']], [], [],
                _plain_epilogue, BF16, "in_proj", w_transposed=True)
        zg = _mm(u2, tm, LANES, LANES, [_layer_wt_spec(l, d, LANES)], [wts['w_mgate_t']],
                 [pl.BlockSpec((None, 1, LANES), lambda i, j: (l, 0, 0))], [wts['mgate_bias']],
                 _bias_epilogue, F32, "in_proj_gates", w_transposed=True)
        return u2, us5, z.reshape(u.shape[0], u.shape[1], n_rest), zg

    u2_lat, us5_lat, z_lat, zg_lat = in_proj(u_lat)
    u2_ctx, us5_ctx, z_ctx, zg_ctx = in_proj(u_ctx)
    s5_l, s5_c = _s5_branch(us5_lat, us5_ctx, b, wts['s5_tables'],
                            wts['s5_w_glu'], wts['s5_b_glu'], l, want_ctx)
    ml_l, ml_c = _mlstm_branch(z_lat, zg_lat, z_ctx, zg_ctx, wts['mlstm_norm_gain'], l,
                               col_q, col_k, col_v, col_o)
    conv_l = _conv_branch(z_lat, wts['conv_w'], l, col_conv)
    pool_l = _pool_branch(z_lat, wts['pool_w'], wts['pool_scale'], l, col_pool)
    y_lat = _merge(u2_lat, (s5_l, conv_l, pool_l, ml_l), wts['w_in_t'], wts['gate_row0'], wts['w_branch'], l)
    if not want_ctx:
        return y_lat, None
    conv_c = _conv_branch(z_ctx, wts['conv_w'], l, col_conv)
    pool_c = _pool_branch(z_ctx, wts['pool_w'], wts['pool_scale'], l, col_pool)
    y_ctx = _merge(u2_ctx, (s5_c, conv_c, pool_c, ml_c), wts['w_in_t'], wts['gate_row0'], wts['w_branch'], l)
    return y_lat, y_ctx


def _out_proj_resid(h, y, gate, w_out, l):
    b, t, d = h.shape
    m = b * t
    tm = min(m, ROW_TILE)
    tn = 512
    out = _mm(y, tm, tn, d, [_layer_w_spec(l, d, tn)], [w_out],
              [pl.BlockSpec((tm, tn), lambda i, j: (i, j)), _gate_spec(gate, tm, t, tn)],
              [h.reshape(m, d), gate], functools.partial(_resid_epilogue, coef=1.0), F32, "out_proj_resid")
    return out.reshape(b, t, d)


def kernel(x, c, ctx, c_ctx, w_ada, b_ada, w_ffn1_in, w_ffn1_out, w_ffn2_in, w_ffn2_out, w_in, s5_a_re, s5_a_im, s5_log_dt, s5_b_re, s5_b_im, s5_c_re, s5_c_im, s5_d, s5_w_glu, s5_b_glu, conv_w, pool_w, pool_scale, mlstm_gate_bias, mlstm_norm_gain, w_branch, w_out, final_gain):
    n_batch, t_lat, d = x.shape
    depth = w_ada.shape[0]
    w = d // 4
    n_main = 8 * w
    n_mgate = 4 * MLSTM_HEADS
    dk = w // MLSTM_HEADS // 2

    row_scale = jnp.ones((w_in.shape[2], 1), F32).at[5 * w:5 * w + w // 2].set(dk ** -0.5)
    w_in_t = (jnp.transpose(w_in, (0, 2, 1)) * row_scale).astype(BF16)
    wts = {
        'w_in_t': w_in_t,
        'gate_row0': n_main + n_mgate,
        'w_mgate_t': jnp.pad(w_in_t[:, n_main:n_main + n_mgate], ((0, 0), (0, LANES - n_mgate), (0, 0))),
        'mgate_bias': jnp.pad(mlstm_gate_bias.reshape(depth, 1, n_mgate).astype(F32),
                              ((0, 0), (0, 0), (0, LANES - n_mgate))),
        'w_branch': w_branch.astype(BF16),
        's5_w_glu': s5_w_glu.astype(BF16),
        's5_b_glu': s5_b_glu.astype(F32).reshape(depth, 1, w),
        'conv_w': conv_w.astype(F32),
        'pool_w': pool_w.astype(BF16),
        'pool_scale': pool_scale.astype(F32).reshape(depth, 1, w),
        'mlstm_norm_gain': mlstm_norm_gain.astype(F32).reshape(depth, 1, w),
        's5_tables': _s5_all_tables(s5_a_re, s5_a_im, s5_log_dt, s5_b_re, s5_b_im, s5_c_re, s5_c_im, s5_d),
    }
    w1i, w1o = w_ffn1_in.astype(BF16), w_ffn1_out.astype(BF16)
    w2i, w2o = w_ffn2_in.astype(BF16), w_ffn2_out.astype(BF16)
    w_o = w_out.astype(BF16)

    c_all = jnp.zeros((8, d), F32).at[:n_batch].set(c).at[n_batch].set(c_ctx)
    mods = _ada_mod(c_all, w_ada, b_ada).reshape(depth, 8, N_MOD, d)

    h, hc = x, ctx
    for l in range(depth):
        last = l == depth - 1
        mod = [mods[l, :n_batch, k][:, None, :] for k in range(N_MOD)]
        modc = [mods[l, n_batch:n_batch + 1, k][:, None, :] for k in range(N_MOD)]
        h = _ffn(h, mod[0], mod[1], mod[2], w1i, w1o, l)
        hc = _ffn(hc, modc[0], modc[1], modc[2], w1i, w1o, l)
        col_major = (l % 2) == 1
        u = _prep(h, mod[3], mod[4])
        uc = _prep(hc, modc[3], modc[4])
        if col_major:
            u = u.reshape(n_batch, t_lat // GRID_W, GRID_W, d).transpose(0, 2, 1, 3).reshape(n_batch, t_lat, d)
        y, yc = _token_mixer(u, uc, not last, wts, l)
        if col_major:
            y = y.reshape(n_batch, GRID_W, t_lat // GRID_W, d).transpose(0, 2, 1, 3).reshape(n_batch * t_lat, d)
        h = _out_proj_resid(h, y, mod[5], w_o, l)
        h = _ffn(h, mod[6], mod[7], mod[8], w2i, w2o, l)
        if not last:
            hc = _out_proj_resid(hc, yc, modc[5], w_o, l)
            hc = _ffn(hc, modc[6], modc[7], modc[8], w2i, w2o, l)
    return _final_norm(h, final_gain)
```

```python
import functools
import math

import jax
import jax.numpy as jnp
from jax import lax
from jax.experimental import pallas as pl
from jax.experimental.pallas import tpu as pltpu

F32 = jnp.float32
BF16 = jnp.bfloat16
EPS = 1e-6
GRID_W = 64
N_MOD = 9
N_BRANCH = 4
S5_P = 16
S5_N = 64
S5_CHUNK = 16
LANES = 128
S5_LANE_GROUPS = LANES // S5_P
POOL_WINDOWS = (2, 4, 8, 16)
MLSTM_HEADS = 4
MLSTM_CHUNK = 128
BF16_ROWS = 16
HALO = BF16_ROWS
ROW_TILE = 1024
VMEM_LIMIT = 56 * 1024 * 1024
HI = lax.Precision.HIGHEST


def _params(n_grid_dims):
    return pltpu.CompilerParams(dimension_semantics=("arbitrary",) * n_grid_dims,
                                vmem_limit_bytes=VMEM_LIMIT)


def _gelu_tanh(y):
    return 0.5 * y * (1.0 + jnp.tanh(math.sqrt(2.0 / math.pi) * (y + 0.044715 * (y * y * y))))


def _log_sigmoid(x):
    return jnp.minimum(x, 0.0) - jnp.log1p(jnp.exp(-jnp.abs(x)))


def _ada_kernel(c_ref, w_ref, b_ref, o_ref):
    c = c_ref[...]
    a = (c * jax.nn.sigmoid(c)).astype(BF16)
    o_ref[0] = jnp.dot(a, w_ref[0].astype(BF16), preferred_element_type=F32) + b_ref[0]


def _ada_mod(c_all, w_ada, b_ada):
    n_layers, d, nd = w_ada.shape
    tn = 1024
    return pl.pallas_call(
        _ada_kernel,
        grid=(n_layers, nd // tn),
        in_specs=[pl.BlockSpec((8, d), lambda l, j: (0, 0)),
                  pl.BlockSpec((1, d, tn), lambda l, j: (l, 0, j)),
                  pl.BlockSpec((1, 1, tn), lambda l, j: (l, 0, j))],
        out_specs=pl.BlockSpec((1, 8, tn), lambda l, j: (l, 0, j)),
        out_shape=jax.ShapeDtypeStruct((n_layers, 8, nd), F32),
        compiler_params=_params(2), name="ada_mod",
    )(c_all, w_ada, b_ada.reshape(n_layers, 1, nd))


def _prep_kernel(h_ref, shift_ref, scale_ref, o_ref):
    x = h_ref[0]
    ms = jnp.mean(x * x, axis=-1, keepdims=True)
    o_ref[0] = (x * lax.rsqrt(ms + EPS) * (1.0 + scale_ref[0]) + shift_ref[0]).astype(o_ref.dtype)


def _prep(h, shift, scale):
    b, t, d = h.shape
    per_batch = shift.shape[0] > 1
    mod_spec = pl.BlockSpec((1, 1, d), lambda bi, i: (bi if per_batch else 0, 0, 0))
    tr = min(t, 512)
    spec = pl.BlockSpec((1, tr, d), lambda bi, i: (bi, i, 0))
    return pl.pallas_call(
        _prep_kernel, grid=(b, t // tr), in_specs=[spec, mod_spec, mod_spec], out_specs=spec,
        out_shape=jax.ShapeDtypeStruct((b, t, d), BF16),
        compiler_params=_params(2), name="prep",
    )(h, shift, scale)


def _final_kernel(h_ref, g_ref, o_ref):
    x = h_ref[0]
    ms = jnp.mean(x * x, axis=-1, keepdims=True)
    o_ref[0] = x * lax.rsqrt(ms + EPS) * g_ref[...]


def _final_norm(h, gain):
    b, t, d = h.shape
    tr = min(t, 512)
    spec = pl.BlockSpec((1, tr, d), lambda bi, i: (bi, i, 0))
    return pl.pallas_call(
        _final_kernel, grid=(b, t // tr),
        in_specs=[spec, pl.BlockSpec((1, d), lambda bi, i: (0, 0))], out_specs=spec,
        out_shape=jax.ShapeDtypeStruct((b, t, d), F32),
        compiler_params=_params(2), name="final_norm",
    )(h, gain.reshape(1, d))


_NT_DIMS = (((1,), (1,)), ((), ()))


def _mm_kernel(*refs, n_w, n_extra, epilogue, w_transposed):
    x_ref = refs[0]
    w_refs = refs[1:1 + n_w]
    extra = refs[1 + n_w:1 + n_w + n_extra]
    o_ref = refs[1 + n_w + n_extra]
    x = x_ref[...]
    if w_transposed:
        accs = [lax.dot_general(x, w[...], _NT_DIMS, preferred_element_type=F32) for w in w_refs]
    else:
        accs = [jnp.dot(x, w[...], preferred_element_type=F32) for w in w_refs]
    o_ref[...] = epilogue(accs, x, extra).astype(o_ref.dtype)


def _mm(x, tm, tn, n_cols, w_specs, w_arrays, extra_specs, extra_arrays, epilogue, out_dtype, name,
        w_transposed=False):
    m, k = x.shape
    return pl.pallas_call(
        functools.partial(_mm_kernel, n_w=len(w_arrays), n_extra=len(extra_arrays), epilogue=epilogue,
                          w_transposed=w_transposed),
        grid=(m // tm, n_cols // tn),
        in_specs=[pl.BlockSpec((tm, k), lambda i, j: (i, 0))] + list(w_specs) + list(extra_specs),
        out_specs=pl.BlockSpec((tm, tn), lambda i, j: (i, j)),
        out_shape=jax.ShapeDtypeStruct((m, n_cols), out_dtype),
        compiler_params=_params(2), name=name,
    )(x, *w_arrays, *extra_arrays)


def _layer_w_spec(l, k, tn, col_block_offset=0):
    return pl.BlockSpec((None, k, tn), lambda i, j: (l, 0, col_block_offset + j))


def _layer_wt_spec(l, k, tn, row_block_offset=0):
    return pl.BlockSpec((None, tn, k), lambda i, j: (l, row_block_offset + j, 0))


def _gate_spec(gate, tm, rows_per_batch, tn):
    per_batch = gate.shape[0] > 1
    return pl.BlockSpec((1, 1, tn), lambda i, j: ((i * tm) // rows_per_batch if per_batch else 0, 0, j))


def _swiglu_epilogue(accs, x, extra):
    a, g = accs
    return g * jax.nn.sigmoid(g) * a


def _resid_epilogue(accs, x, extra, *, coef):
    h_ref, gate_ref = extra
    return h_ref[...] + (coef * gate_ref[0]) * accs[0]


def _plain_epilogue(accs, x, extra):
    return accs[0]


def _bias_epilogue(accs, x, extra):
    return accs[0] + extra[0][...]


def _glu_epilogue(accs, x, extra):
    return x.astype(F32) * jax.nn.sigmoid(accs[0] + extra[0][...])


def _ffn(h, shift, scale, gate, w_in, w_out, l):
    b, t, d = h.shape
    m = b * t
    d_ff = w_out.shape[1]
    tm = min(m, ROW_TILE)
    tn = 512
    xn = _prep(h, shift, scale).reshape(m, d)
    act = _mm(xn, tm, tn, d_ff,
              [_layer_w_spec(l, d, tn), _layer_w_spec(l, d, tn, d_ff // tn)], [w_in, w_in],
              [], [], _swiglu_epilogue, BF16, "ffn_in")
    out = _mm(act, tm, tn, d,
              [_layer_w_spec(l, d_ff, tn)], [w_out],
              [pl.BlockSpec((tm, tn), lambda i, j: (i, j)), _gate_spec(gate, tm, t, tn)],
              [h.reshape(m, d), gate],
              functools.partial(_resid_epilogue, coef=0.5), F32, "ffn_out")
    return out.reshape(b, t, d)


def _merge_kernel(u_ref, *refs):
    b_refs = refs[:N_BRANCH]
    wg_refs = refs[N_BRANCH:2 * N_BRANCH]
    wb_ref, o_ref = refs[2 * N_BRANCH], refs[2 * N_BRANCH + 1]
    u = u_ref[...]
    acc = None
    for k in range(N_BRANCH):
        gate = jax.nn.sigmoid(lax.dot_general(u, wg_refs[k][0], _NT_DIMS, preferred_element_type=F32))
        val = gate * jnp.dot(b_refs[k][...], wb_ref[k], preferred_element_type=F32)
        acc = val if acc is None else acc + val
    o_ref[...] = acc.astype(o_ref.dtype)


def _merge(u, branches, w_in_t, gate_row0, w_branch, l):
    m, d = u.shape
    w = branches[0].shape[1]
    tm = min(m, ROW_TILE)
    tn = 256
    assert gate_row0 % BF16_ROWS == 0, "gate rows must start on a packed sublane tile"
    br_spec = pl.BlockSpec((tm, w), lambda i, j: (i, 0), pipeline_mode=pl.Buffered(1))

    def gate_spec(k):
        return pl.BlockSpec((pl.Element(1), pl.Element(tn), pl.Element(d)),
                            lambda i, j: (l, pl.multiple_of(gate_row0 + k * d + j * tn, BF16_ROWS), 0))

    return pl.pallas_call(
        _merge_kernel, grid=(m // tm, d // tn),
        in_specs=[pl.BlockSpec((tm, d), lambda i, j: (i, 0))] + [br_spec] * N_BRANCH
                 + [gate_spec(k) for k in range(N_BRANCH)]
                 + [pl.BlockSpec((None, N_BRANCH, w, tn), lambda i, j: (l, 0, 0, j))],
        out_specs=pl.BlockSpec((tm, tn), lambda i, j: (i, j)),
        out_shape=jax.ShapeDtypeStruct((m, d), BF16),
        compiler_params=_params(2), name="merge",
    )(u, *branches, *([w_in_t] * N_BRANCH), w_branch)


def _s5_factors(a_re, a_im, log_dt, b_re, b_im, c_re, c_im):
    a_re, a_im = a_re.astype(F32), a_im.astype(F32)
    n_groups = a_re.shape[1]
    lc, p, n, gl = S5_CHUNK, S5_P, S5_N, S5_LANE_GROUPS
    nb = n_groups // gl
    dt = jnp.exp(log_dt.astype(F32))[:, :, None]
    ks = jnp.arange(lc + 1, dtype=F32)[:, None, None, None]
    mag = jnp.exp(ks * (dt * a_re))
    pw = jnp.stack([mag * jnp.cos(ks * (dt * a_im)), mag * jnp.sin(ks * (dt * a_im))])
    lam_re, lam_im = pw[0, 1], pw[1, 1]
    den = a_re * a_re + a_im * a_im
    z_re = ((lam_re - 1.0) * a_re + lam_im * a_im) / den
    z_im = (lam_im * a_re - (lam_re - 1.0) * a_im) / den
    b_re, b_im = b_re.astype(F32), b_im.astype(F32)
    bb = jnp.stack([z_re[..., None] * b_re - z_im[..., None] * b_im,
                    z_re[..., None] * b_im + z_im[..., None] * b_re])
    cc = jnp.stack([c_re.astype(F32), c_im.astype(F32)])
    same = jnp.eye(gl, dtype=F32)
    cb = cc.reshape(2, 2, nb, gl, p, n).transpose(2, 1, 0, 3, 5, 4)
    cb = (cb[:, :, :, :, :, None, :] * same[None, None, None, :, None, :, None]).reshape(nb, 2, 2, gl * n, gl * p)
    bd = bb.reshape(2, 2, nb, gl, n, p).transpose(2, 1, 0, 3, 5, 4)
    bd = (bd[:, :, :, :, :, None, :] * same[None, None, None, :, None, :, None]).reshape(nb, 2, 2, gl * p, gl * n)
    pwr = pw.reshape(2, lc + 1, 2, nb, gl * n).transpose(3, 2, 0, 1, 4)
    pwc = pwr.transpose(0, 1, 2, 4, 3)
    return cb, bd, pwr, pwc


def _s5_tables_kernel(cb_ref, bd_ref, pwr_ref, pwc_ref, skip_ref, toep_ref, ec_ref, bc_ref, k_ref):
    s = pl.program_id(1)
    lc = S5_CHUNK
    ncol = pwc_ref.shape[3]

    def column(d, comp, e):
        lane = lax.broadcasted_iota(jnp.int32, (ncol, pwc_ref.shape[4]), 1)
        return jnp.sum(jnp.where(lane == e, pwc_ref[0, d, comp], 0.0), axis=1, keepdims=True)

    def c_times_power(d, e):
        pr, pi = column(d, 0, e), column(d, 1, e)
        return cb_ref[0, d, 0] * pr - cb_ref[0, d, 1] * pi, cb_ref[0, d, 0] * pi + cb_ref[0, d, 1] * pr

    @pl.when(s == 0)
    def _():
        for d in range(2):
            x_cat = jnp.concatenate([bd_ref[0, d, 0], -bd_ref[0, d, 1]], axis=1)
            for tau in range(lc):
                yr, yi = c_times_power(d, tau)
                k_ref[d, tau] = jnp.dot(x_cat, jnp.concatenate([yr, yi], axis=0), precision=HI,
                                        preferred_element_type=F32)

    rows = lax.broadcasted_iota(jnp.int32, (LANES, LANES), 0)
    cols = lax.broadcasted_iota(jnp.int32, (LANES, LANES), 1)
    diag = jnp.where(rows == cols, skip_ref[0], 0.0)
    for r in range(lc):
        blk = (jnp.where(s >= r, k_ref[0, jnp.maximum(s - r, 0)], 0.0)
               + jnp.where(s <= r, k_ref[1, jnp.maximum(r - s, 0)], 0.0)
               + jnp.where(s == r, diag, 0.0))
        toep_ref[0, r * LANES:(r + 1) * LANES, :] = blk.astype(toep_ref.dtype)
    for d in range(2):
        yr, yi = c_times_power(d, (s + 1) if d == 0 else (lc - s))
        ec_ref[0, d * ncol:(d + 1) * ncol, :] = yr.astype(ec_ref.dtype)
        ec_ref[0, (2 + d) * ncol:(3 + d) * ncol, :] = (-yi).astype(ec_ref.dtype)
        f = (lc - 1 - s) if d == 0 else s
        fr, fi = pwr_ref[0, d, 0, pl.ds(f, 1), :], pwr_ref[0, d, 1, pl.ds(f, 1), :]
        bc_ref[0, :, d * ncol:(d + 1) * ncol] = (bd_ref[0, d, 0] * fr - bd_ref[0, d, 1] * fi).astype(bc_ref.dtype)
        bc_ref[0, :, (2 + d) * ncol:(3 + d) * ncol] = (bd_ref[0, d, 0] * fi + bd_ref[0, d, 1] * fr).astype(bc_ref.dtype)


def _s5_all_tables(a_re, a_im, log_dt, b_re, b_im, c_re, c_im, d_skip):
    cb, bd, pwr, pwc = jax.vmap(_s5_factors)(a_re, a_im, log_dt, b_re, b_im, c_re, c_im)
    depth, nb = cb.shape[:2]
    lc = S5_CHUNK
    nk = depth * nb
    flat = lambda t: t.reshape((nk,) + t.shape[2:])
    cb, bd, pwr, pwc = flat(cb), flat(bd), flat(pwr), flat(pwc)
    ncol, nrow = cb.shape[3], cb.shape[4]
    skip = d_skip.astype(F32).reshape(nk, 1, nrow)
    whole = lambda t: pl.BlockSpec((1,) + t.shape[1:], lambda k, s: (k,) + (0,) * (t.ndim - 1))
    toep, ec, bc = pl.pallas_call(
        _s5_tables_kernel, grid=(nk, lc),
        in_specs=[whole(cb), whole(bd), whole(pwr), whole(pwc), whole(skip)],
        out_specs=[pl.BlockSpec((1, lc * nrow, nrow), lambda k, s: (k, 0, s)),
                   pl.BlockSpec((1, 4 * ncol, nrow), lambda k, s: (k, 0, s)),
                   pl.BlockSpec((1, nrow, 4 * ncol), lambda k, s: (k, s, 0))],
        out_shape=[jax.ShapeDtypeStruct((nk, lc * nrow, lc * nrow), BF16),
                   jax.ShapeDtypeStruct((nk, 4 * ncol, lc * nrow), BF16),
                   jax.ShapeDtypeStruct((nk, lc * nrow, 4 * ncol), BF16)],
        scratch_shapes=[pltpu.VMEM((2, lc, nrow, nrow), F32)],
        compiler_params=_params(2), name="s5_tables",
    )(cb, bd, pwr, pwc, skip)
    unflat = lambda t: t.reshape((depth, nb) + t.shape[1:])
    groups_states = nb * ncol
    lam_re = pwr[:, :, 0, lc].reshape(depth, nb, 2, ncol).transpose(0, 2, 1, 3).reshape(depth, 2, 1, groups_states)
    lam_im = pwr[:, :, 1, lc].reshape(depth, nb, 2, ncol).transpose(0, 2, 1, 3).reshape(depth, 2, 1, groups_states)
    return unflat(toep), unflat(ec), unflat(bc), lam_re, lam_im


def _s5_proj_kernel(x_ref, w_ref, o_ref, acc_ref):
    acc = lax.dot_general(x_ref[...], w_ref[...], _NT_DIMS, preferred_element_type=F32)
    n_rows = o_ref.shape[1]
    for blk in range(o_ref.shape[0]):
        acc_ref[blk] = acc[:, blk * LANES:(blk + 1) * LANES]
        for r in range(S5_CHUNK):
            rows = acc_ref[blk, pl.ds(r, n_rows, stride=S5_CHUNK), :]
            o_ref[blk, :, r * LANES:(r + 1) * LANES] = rows.astype(o_ref.dtype)


def _s5_proj(u2, w_main_t, l, w):
    m, d = u2.shape
    tm = min(m, ROW_TILE)
    nb = w // LANES
    return pl.pallas_call(
        _s5_proj_kernel, grid=(m // tm,),
        in_specs=[pl.BlockSpec((tm, d), lambda i: (i, 0)), pl.BlockSpec((None, w, d), lambda i: (l, 0, 0))],
        out_specs=pl.BlockSpec((nb, tm // S5_CHUNK, S5_CHUNK * LANES), lambda i: (0, i, 0)),
        out_shape=jax.ShapeDtypeStruct((nb, m // S5_CHUNK, S5_CHUNK * LANES), BF16),
        scratch_shapes=[pltpu.VMEM((nb, tm, LANES), F32)],
        compiler_params=_params(1), name="s5_proj",
    )(u2, w_main_t)


def _s5_in_kernel(u_ref, w_ref, re_ref, im_ref):
    acc = jnp.dot(u_ref[0], w_ref[0], preferred_element_type=F32)
    w = re_ref.shape[2]
    re_ref[0] = acc[:, 0:w]
    re_ref[1] = acc[:, w:2 * w]
    im_ref[0] = acc[:, 2 * w:3 * w]
    im_ref[1] = acc[:, 3 * w:4 * w]


def _s5_scan_kernel(lre, lim, cre, cim, lr_ref, li_ref, o_lre, o_lim, o_cre, o_cim, *, n_batch, n_lat, n_ctx):
    d = pl.program_id(0)
    lr = lr_ref[0]
    li = li_ref[0]

    def run(re_ref, im_ref, ore_ref, oim_ref, n_chunks, carry):
        def body(j, carry):
            idx = jnp.where(d == 0, j, n_chunks - 1 - j)
            new = []
            for bi in range(n_batch):
                xr, xi = carry[bi]
                row = bi * n_chunks + idx
                ore_ref[0, pl.ds(row, 1), :] = xr
                oim_ref[0, pl.ds(row, 1), :] = xi
                ur = re_ref[0, pl.ds(row, 1), :]
                ui = im_ref[0, pl.ds(row, 1), :]
                new.append((lr * xr - li * xi + ur, lr * xi + li * xr + ui))
            return tuple(new)

        return lax.fori_loop(0, n_chunks, body, carry)

    zero = jnp.zeros(lr.shape, F32)
    carry = run(cre, cim, o_cre, o_cim, n_ctx, tuple((zero, zero) for _ in range(n_batch)))
    run(lre, lim, o_lre, o_lim, n_lat, carry)


def _s5_out_kernel(u_ref, xr_ref, xi_ref, t_ref, e_ref, o_ref):
    xcat = jnp.concatenate([xr_ref[0], xr_ref[1], xi_ref[0], xi_ref[1]], axis=1).astype(BF16)
    y = (jnp.dot(u_ref[0], t_ref[0], preferred_element_type=F32)
         + jnp.dot(xcat, e_ref[0], preferred_element_type=F32))
    o_ref[0] = _gelu_tanh(y).astype(o_ref.dtype)


def _s5_glu_kernel(y_ref, w_ref, b_ref, o_ref, g_ref):
    n_rows = y_ref.shape[1]
    for blk in range(y_ref.shape[0]):
        for s in range(S5_CHUNK):
            g_ref[blk, pl.ds(s, n_rows, stride=S5_CHUNK), :] = y_ref[blk, :, s * LANES:(s + 1) * LANES].astype(F32)
    g = jnp.concatenate([g_ref[blk] for blk in range(y_ref.shape[0])], axis=1)
    z = jnp.dot(g.astype(BF16), w_ref[...], preferred_element_type=F32) + b_ref[...]
    o_ref[...] = (g * jax.nn.sigmoid(z)).astype(o_ref.dtype)


def _s5_branch(uc_lat, uc_ctx, n_batch, tables, w_glu, b_glu, l, want_ctx):
    toep, ec, bc, lam_re, lam_im = tables
    nb, r_lat, cw = uc_lat.shape
    r_ctx = uc_ctx.shape[1]
    gn = lam_re.shape[3]
    sw = gn // nb
    w = nb * LANES

    def chunk_rows(r):
        return min(r, 512)

    def state_in(uc):
        r = uc.shape[1]
        rt = chunk_rows(r)
        x_spec = pl.BlockSpec((2, rt, sw), lambda k, i: (0, i, k))
        x_shape = jax.ShapeDtypeStruct((2, r, gn), F32)
        return pl.pallas_call(
            _s5_in_kernel, grid=(nb, r // rt),
            in_specs=[pl.BlockSpec((1, rt, cw), lambda k, i: (k, i, 0)),
                      pl.BlockSpec((None, 1, cw, 4 * sw), lambda k, i: (l, k, 0, 0))],
            out_specs=[x_spec, x_spec], out_shape=[x_shape, x_shape],
            compiler_params=_params(2), name="s5_in",
        )(uc, bc)

    lre, lim = state_in(uc_lat)
    cre, cim = state_in(uc_ctx)
    lb = 512
    lat_blk = pl.BlockSpec((1, r_lat, lb), lambda d, j: (d, 0, j))
    ctx_blk = pl.BlockSpec((1, r_ctx, lb), lambda d, j: (d, 0, j))
    lam_spec = pl.BlockSpec((None, 1, 1, lb), lambda d, j: (l, d, 0, j))
    lat_shape = jax.ShapeDtypeStruct((2, r_lat, gn), F32)
    ctx_shape = jax.ShapeDtypeStruct((2, r_ctx, gn), F32)
    xl_re, xl_im, xc_re, xc_im = pl.pallas_call(
        functools.partial(_s5_scan_kernel, n_batch=n_batch, n_lat=r_lat // n_batch, n_ctx=r_ctx // n_batch),
        grid=(2, gn // lb),
        in_specs=[lat_blk, lat_blk, ctx_blk, ctx_blk, lam_spec, lam_spec],
        out_specs=[lat_blk, lat_blk, ctx_blk, ctx_blk],
        out_shape=[lat_shape, lat_shape, ctx_shape, ctx_shape],
        compiler_params=_params(2), name="s5_scan",
    )(lre, lim, cre, cim, lam_re, lam_im)

    def readout(uc, x_re, x_im):
        r = uc.shape[1]
        rt = chunk_rows(r)
        hw = cw // 2
        x_spec = pl.BlockSpec((2, rt, sw), lambda k, h, i: (0, i, k))
        return pl.pallas_call(
            _s5_out_kernel, grid=(nb, 2, r // rt),
            in_specs=[pl.BlockSpec((1, rt, cw), lambda k, h, i: (k, i, 0)), x_spec, x_spec,
                      pl.BlockSpec((None, 1, cw, hw), lambda k, h, i: (l, k, 0, h)),
                      pl.BlockSpec((None, 1, 4 * sw, hw), lambda k, h, i: (l, k, 0, h))],
            out_specs=pl.BlockSpec((1, rt, hw), lambda k, h, i: (k, i, h)),
            out_shape=jax.ShapeDtypeStruct((nb, r, cw), BF16),
            compiler_params=_params(3), name="s5_out",
        )(uc, x_re, x_im, toep, ec)

    def glu(y):
        m = y.shape[1] * S5_CHUNK
        tm = min(m, ROW_TILE)
        return pl.pallas_call(
            _s5_glu_kernel, grid=(m // tm,),
            in_specs=[pl.BlockSpec((nb, tm // S5_CHUNK, cw), lambda i: (0, i, 0)),
                      pl.BlockSpec((None, w, w), lambda i: (l, 0, 0)),
                      pl.BlockSpec((None, 1, w), lambda i: (l, 0, 0))],
            out_specs=pl.BlockSpec((tm, w), lambda i: (i, 0)),
            out_shape=jax.ShapeDtypeStruct((m, w), BF16),
            scratch_shapes=[pltpu.VMEM((nb, tm, LANES), F32)],
            compiler_params=_params(1), name="s5_glu",
        )(y, w_glu, b_glu)

    out_lat = glu(readout(uc_lat, xl_re, xl_im))
    out_ctx = glu(readout(uc_ctx, xc_re, xc_im)) if want_ctx else None
    return out_lat, out_ctx


def _conv_kernel(hm, hp, hn, cm, cp, cn, bm, w_ref, o_ref, *, n_tiles):
    i = pl.program_id(1)
    xg = hm[0].astype(F32) * cm[0].astype(F32)
    tm = xg.shape[0]
    x_before = jnp.where(i > 0, hp[0, HALO - 1:HALO].astype(F32) * cp[0, HALO - 1:HALO].astype(F32), 0.0)
    x_after = jnp.where(i < n_tiles - 1, hn[0, 0:1].astype(F32) * cn[0, 0:1].astype(F32), 0.0)
    row = lax.broadcasted_iota(jnp.int32, xg.shape, 0)
    prev = jnp.where(row == 0, x_before, pltpu.roll(xg, 1, axis=0))
    nxt = jnp.where(row == tm - 1, x_after, pltpu.roll(xg, tm - 1, axis=0))
    w = w_ref[...]
    y = w[0:1] * prev + w[1:2] * xg + w[2:3] * nxt
    o_ref[0] = (bm[0].astype(F32) * y).astype(o_ref.dtype)


def _halo_specs(tm, width, col_block, n_halo_blocks):
    hb = tm // HALO
    main = pl.BlockSpec((1, tm, width), lambda b, i, c: (b, i, col_block + c))
    before = pl.BlockSpec((1, HALO, width), lambda b, i, c: (b, jnp.maximum(i * hb - 1, 0), col_block + c))
    after = pl.BlockSpec((1, HALO, width),
                         lambda b, i, c: (b, jnp.minimum((i + 1) * hb, n_halo_blocks - 1), col_block + c))
    return main, before, after


def _conv_branch(z, conv_w, l, col0):
    b, t, _ = z.shape
    w = conv_w.shape[2]
    tm = min(t, ROW_TILE)
    tc = 512
    n_tiles = t // tm
    cb = col0 // tc
    h_specs = _halo_specs(tm, tc, cb, t // HALO)
    bg_spec = _halo_specs(tm, tc, cb + w // tc, t // HALO)[0]
    c_specs = _halo_specs(tm, tc, cb + 2 * (w // tc), t // HALO)
    out = pl.pallas_call(
        functools.partial(_conv_kernel, n_tiles=n_tiles), grid=(b, n_tiles, w // tc),
        in_specs=[*h_specs, *c_specs, bg_spec, pl.BlockSpec((None, 3, tc), lambda bi, i, c: (l, 0, c))],
        out_specs=pl.BlockSpec((1, tm, tc), lambda bi, i, c: (bi, i, c)),
        out_shape=jax.ShapeDtypeStruct((b, t, w), BF16),
        compiler_params=_params(3), name="conv",
    )(z, z, z, z, z, z, z, conv_w)
    return out.reshape(b * t, w)


def _pool_kernel(um, up, un, w_ref, s_ref, o_ref, *, n_tiles, seq_len):
    i = pl.program_id(1)
    gi = pl.program_id(2)
    xm = um[0]
    tm = xm.shape[0]
    win = jnp.left_shift(POOL_WINDOWS[0], gi)
    half = win // 2
    x_before = jnp.where(i > 0, up[0], jnp.zeros_like(up[0]))
    x_after = jnp.where(i < n_tiles - 1, un[0], jnp.zeros_like(un[0]))
    ext = jnp.concatenate([x_before, xm, x_after], axis=0)
    s = lax.broadcasted_iota(jnp.int32, (tm, tm + 2 * HALO), 0)
    r = lax.broadcasted_iota(jnp.int32, (tm, tm + 2 * HALO), 1) - HALO
    off = r - s
    band = jnp.where(off >= -half, jnp.where(off < win - half, 1.0, 0.0), 0.0).astype(BF16)
    wsum = jnp.dot(band, ext, preferred_element_type=F32)
    t = i * tm + lax.broadcasted_iota(jnp.int32, (tm, 1), 0)
    cnt = (jnp.minimum(t + win - half, seq_len) - jnp.maximum(t - half, 0)).astype(F32)
    p = wsum / cnt - xm.astype(F32)
    y = jnp.dot(p.astype(BF16), w_ref[...], preferred_element_type=F32) * s_ref[...]
    o_ref[0] = y.astype(o_ref.dtype)


def _pool_branch(z, pool_w, pool_scale, l, col0):
    b, t, _ = z.shape
    n_groups, gw = pool_w.shape[1], pool_w.shape[2]
    tm = min(t, 512)
    n_tiles = t // tm
    specs = _halo_specs(tm, gw, col0 // gw, t // HALO)
    out = pl.pallas_call(
        functools.partial(_pool_kernel, n_tiles=n_tiles, seq_len=t), grid=(b, n_tiles, n_groups),
        in_specs=[*specs, pl.BlockSpec((None, None, gw, gw), lambda bi, i, c: (l, c, 0, 0)),
                  pl.BlockSpec((None, 1, gw), lambda bi, i, c: (l, 0, c))],
        out_specs=pl.BlockSpec((1, tm, gw), lambda bi, i, c: (bi, i, c)),
        out_shape=jax.ShapeDtypeStruct((b, t, n_groups * gw), BF16),
        compiler_params=_params(3), name="pool",
    )(z, z, z, pool_w, pool_scale)
    return out.reshape(b * t, n_groups * gw)


def _mlstm_chunk(q, k, v, li_col, lf_col, li_row, lf_row, state, reverse):
    c_mat, n_row, m = state
    lc = q.shape[0]
    ti = lax.broadcasted_iota(jnp.int32, (lc, lc), 0)
    si = lax.broadcasted_iota(jnp.int32, (lc, lc), 1)
    seen = (si >= ti) if reverse else (si <= ti)
    seen_t = (ti >= si) if reverse else (ti <= si)
    bcum_col = jnp.sum(jnp.where(seen, lf_row, 0.0), axis=1, keepdims=True)
    bcum_row = jnp.sum(jnp.where(seen_t, lf_col, 0.0), axis=0, keepdims=True)
    b_last = jnp.sum(lf_row, axis=1, keepdims=True)
    dmat = bcum_col - bcum_row + li_row
    m_inter = bcum_col + m
    m_t = jnp.maximum(jnp.max(jnp.where(seen, dmat, -1e30), axis=1, keepdims=True), m_inter)
    scores = lax.dot_general(q, k, (((1,), (1,)), ((), ())), preferred_element_type=F32)
    wgt = jnp.where(seen, scores * jnp.exp(dmat - m_t), 0.0)
    decay = jnp.exp(m_inter - m_t)
    num = (jnp.dot(wgt.astype(BF16), v, preferred_element_type=F32)
           + decay * jnp.dot(q, c_mat.astype(BF16), preferred_element_type=F32))
    den = (jnp.sum(wgt, axis=1, keepdims=True)
           + decay * jnp.sum(q.astype(F32) * n_row, axis=1, keepdims=True))
    h = num / jnp.maximum(jnp.abs(den), jnp.exp(-m_t))
    g_col = b_last - bcum_col + li_col
    m_new = jnp.maximum(b_last + m, jnp.max(g_col, axis=0, keepdims=True))
    carry = jnp.exp(b_last + m - m_new)
    wk = k.astype(F32) * jnp.exp(g_col - m_new)
    c_new = carry * c_mat + jnp.dot(wk.T.astype(BF16), v, preferred_element_type=F32)
    n_new = carry * n_row + jnp.sum(wk, axis=0, keepdims=True)
    return h, (c_new, n_new, m_new)


def _mlstm_kernel(ql, kl, vl, ol, gcl, grl, qc, kc, vc, oc, gcc, grc, gain_ref, out_l, out_c, h_l, h_c):
    lc = MLSTM_CHUNK
    dk, dv = ql.shape[2], vl.shape[2]

    def run(refs, h_ref, n_chunks, states):
        q_ref, k_ref, v_ref, gc_ref, gr_ref = refs

        def one(c, st, d):
            t0 = pl.multiple_of(c * lc, lc)
            gcol = gc_ref[0, 0, pl.ds(t0, lc), :]
            li_col = gcol[:, d:d + 1]
            lf_col = _log_sigmoid(gcol[:, 2 + d:3 + d])
            li_row = gr_ref[0, 0, d, pl.ds(c, 1), :]
            lf_row = _log_sigmoid(gr_ref[0, 0, 2 + d, pl.ds(c, 1), :])
            h, st = _mlstm_chunk(q_ref[0, pl.ds(t0, lc), :], k_ref[0, pl.ds(t0, lc), :],
                                 v_ref[0, pl.ds(t0, lc), :], li_col, lf_col, li_row, lf_row, st, d == 1)
            return t0, h, st

        def make_body(second_visit):
            def body(j, sts):
                t_f, h_f, st_f = one(j, sts[0], 0)
                t_b, h_b, st_b = one(n_chunks - 1 - j, sts[1], 1)
                for t0, h in ((t_f, h_f), (t_b, h_b)):
                    if second_visit:
                        h_ref[pl.ds(t0, lc), :] += h
                    else:
                        h_ref[pl.ds(t0, lc), :] = h
                return (st_f, st_b)
            return body

        half = n_chunks // 2
        states = lax.fori_loop(0, half, make_body(False), states)
        return lax.fori_loop(half, n_chunks, make_body(True), states)

    lat = (ql, kl, vl, gcl, grl)
    ctx = (qc, kc, vc, gcc, grc)
    n_lat, n_ctx = ql.shape[1] // lc, qc.shape[1] // lc
    zero = (jnp.zeros((dk, dv), F32), jnp.zeros((1, dk), F32), jnp.zeros((1, 1), F32))
    states = run(ctx, h_c, n_ctx, (zero, zero))
    run(lat, h_l, n_lat, states)

    gain = gain_ref[...]

    def readout(h_ref, o_ref, out_ref):
        t = h_ref.shape[0]
        tile = min(t, 512)

        def body(i, _):
            t0 = pl.multiple_of(i * tile, tile)
            h = h_ref[pl.ds(t0, tile), :]
            hn = h * lax.rsqrt(jnp.mean(h * h, axis=-1, keepdims=True) + EPS) * gain
            og = jax.nn.sigmoid(o_ref[0, pl.ds(t0, tile), :].astype(F32))
            out_ref[0, pl.ds(t0, tile), :] = (hn * og).astype(out_ref.dtype)
            return 0

        lax.fori_loop(0, t // tile, body, 0)

    readout(h_l, ol, out_l)
    readout(h_c, oc, out_c)


def _mlstm_branch(z_lat, zg_lat, z_ctx, zg_ctx, norm_gain, l, col_q, col_k, col_v, col_o):
    b, t, _ = z_lat.shape
    tc = z_ctx.shape[1]
    nh = MLSTM_HEADS
    w = norm_gain.shape[2]
    dv = w // nh
    dk = dv // 2
    lc = MLSTM_CHUNK
    assert (t // lc) % 2 == 0 and (tc // lc) % 2 == 0, "the two-direction loop pairs chunks"

    def gate_layouts(zg, tt):
        g = zg[:, :4 * nh].reshape(b, tt, 4, nh)
        col = g.transpose(0, 3, 1, 2)
        row = g.transpose(0, 3, 2, 1).reshape(b, nh, 4, tt // lc, lc)
        return col, row

    gcl, grl = gate_layouts(zg_lat, t)
    gcc, grc = gate_layouts(zg_ctx, tc)
    once = pl.Buffered(1)

    def specs(tt):
        return [pl.BlockSpec((1, tt, dk), lambda bi, h: (bi, 0, col_q // dk + h), pipeline_mode=once),
                pl.BlockSpec((1, tt, dk), lambda bi, h: (bi, 0, col_k // dk + h), pipeline_mode=once),
                pl.BlockSpec((1, tt, dv), lambda bi, h: (bi, 0, col_v // dv + h), pipeline_mode=once),
                pl.BlockSpec((1, tt, dv), lambda bi, h: (bi, 0, col_o // dv + h), pipeline_mode=once),
                pl.BlockSpec((1, 1, tt, 4), lambda bi, h: (bi, h, 0, 0), pipeline_mode=once),
                pl.BlockSpec((1, 1, 4, tt // lc, lc), lambda bi, h: (bi, h, 0, 0, 0))]

    out_l, out_c = pl.pallas_call(
        _mlstm_kernel, grid=(b, nh),
        in_specs=specs(t) + specs(tc) + [pl.BlockSpec((None, 1, dv), lambda bi, h: (l, 0, h))],
        out_specs=[pl.BlockSpec((1, t, dv), lambda bi, h: (bi, 0, h)),
                   pl.BlockSpec((1, tc, dv), lambda bi, h: (bi, 0, h))],
        out_shape=[jax.ShapeDtypeStruct((b, t, w), BF16), jax.ShapeDtypeStruct((b, tc, w), BF16)],
        scratch_shapes=[pltpu.VMEM((t, dv), F32), pltpu.VMEM((tc, dv), F32)],
        compiler_params=_params(2), name="mlstm",
    )(z_lat, z_lat, z_lat, z_lat, gcl, grl, z_ctx, z_ctx, z_ctx, z_ctx, gcc, grc, norm_gain)
    return out_l.reshape(b * t, w), out_c.reshape(b * tc, w)


def _token_mixer(u_lat, u_ctx, want_ctx, wts, l):
    b, t, d = u_lat.shape
    tc = u_ctx.shape[1]
    w = d // 4
    n_rest = 7 * w
    col_conv, col_pool, col_q, col_k, col_v, col_o = 0, 3 * w, 4 * w, 4 * w + w // 2, 5 * w, 6 * w

    def in_proj(u):
        m = u.shape[0] * u.shape[1]
        u2 = u.reshape(m, d)
        tm = min(m, ROW_TILE)
        us5 = _s5_proj(u2, wts['w_in_t'], l, w)
        z = _mm(u2, tm, w, n_rest, [_layer_wt_spec(l, d, w, 1)], [wts['w_in_t']], [], [],
                _plain_epilogue, BF16, "in_proj", w_transposed=True)
        zg = _mm(u2, tm, LANES, LANES, [_layer_wt_spec(l, d, LANES)], [wts['w_mgate_t']],
                 [pl.BlockSpec((None, 1, LANES), lambda i, j: (l, 0, 0))], [wts['mgate_bias']],
                 _bias_epilogue, F32, "in_proj_gates", w_transposed=True)
        return u2, us5, z.reshape(u.shape[0], u.shape[1], n_rest), zg

    u2_lat, us5_lat, z_lat, zg_lat = in_proj(u_lat)
    u2_ctx, us5_ctx, z_ctx, zg_ctx = in_proj(u_ctx)
    s5_l, s5_c = _s5_branch(us5_lat, us5_ctx, b, wts['s5_tables'],
                            wts['s5_w_glu'], wts['s5_b_glu'], l, want_ctx)
    ml_l, ml_c = _mlstm_branch(z_lat, zg_lat, z_ctx, zg_ctx, wts['mlstm_norm_gain'], l,
                               col_q, col_k, col_v, col_o)
    conv_l = _conv_branch(z_lat, wts['conv_w'], l, col_conv)
    pool_l = _pool_branch(z_lat, wts['pool_w'], wts['pool_scale'], l, col_pool)
    y_lat = _merge(u2_lat, (s5_l, conv_l, pool_l, ml_l), wts['w_in_t'], wts['gate_row0'], wts['w_branch'], l)
    if not want_ctx:
        return y_lat, None
    conv_c = _conv_branch(z_ctx, wts['conv_w'], l, col_conv)
    pool_c = _pool_branch(z_ctx, wts['pool_w'], wts['pool_scale'], l, col_pool)
    y_ctx = _merge(u2_ctx, (s5_c, conv_c, pool_c, ml_c), wts['w_in_t'], wts['gate_row0'], wts['w_branch'], l)
    return y_lat, y_ctx


def _out_proj_resid(h, y, gate, w_out, l):
    b, t, d = h.shape
    m = b * t
    tm = min(m, ROW_TILE)
    tn = 512
    out = _mm(y, tm, tn, d, [_layer_w_spec(l, d, tn)], [w_out],
              [pl.BlockSpec((tm, tn), lambda i, j: (i, j)), _gate_spec(gate, tm, t, tn)],
              [h.reshape(m, d), gate], functools.partial(_resid_epilogue, coef=1.0), F32, "out_proj_resid")
    return out.reshape(b, t, d)


def kernel(x, c, ctx, c_ctx, w_ada, b_ada, w_ffn1_in, w_ffn1_out, w_ffn2_in, w_ffn2_out, w_in, s5_a_re, s5_a_im, s5_log_dt, s5_b_re, s5_b_im, s5_c_re, s5_c_im, s5_d, s5_w_glu, s5_b_glu, conv_w, pool_w, pool_scale, mlstm_gate_bias, mlstm_norm_gain, w_branch, w_out, final_gain):
    n_batch, t_lat, d = x.shape
    depth = w_ada.shape[0]
    w = d // 4
    n_main = 8 * w
    n_mgate = 4 * MLSTM_HEADS
    dk = w // MLSTM_HEADS // 2

    row_scale = jnp.ones((w_in.shape[2], 1), F32).at[5 * w:5 * w + w // 2].set(dk ** -0.5)
    w_in_t = (jnp.transpose(w_in, (0, 2, 1)) * row_scale).astype(BF16)
    wts = {
        'w_in_t': w_in_t,
        'gate_row0': n_main + n_mgate,
        'w_mgate_t': jnp.pad(w_in_t[:, n_main:n_main + n_mgate], ((0, 0), (0, LANES - n_mgate), (0, 0))),
        'mgate_bias': jnp.pad(mlstm_gate_bias.reshape(depth, 1, n_mgate).astype(F32),
                              ((0, 0), (0, 0), (0, LANES - n_mgate))),
        'w_branch': w_branch.astype(BF16),
        's5_w_glu': s5_w_glu.astype(BF16),
        's5_b_glu': s5_b_glu.astype(F32).reshape(depth, 1, w),
        'conv_w': conv_w.astype(F32),
        'pool_w': pool_w.astype(BF16),
        'pool_scale': pool_scale.astype(F32).reshape(depth, 1, w),
        'mlstm_norm_gain': mlstm_norm_gain.astype(F32).reshape(depth, 1, w),
        's5_tables': _s5_all_tables(s5_a_re, s5_a_im, s5_log_dt, s5_b_re, s5_b_im, s5_c_re, s5_c_im, s5_d),
    }
    w1i, w1o = w_ffn1_in.astype(BF16), w_ffn1_out.astype(BF16)
    w2i, w2o = w_ffn2_in.astype(BF16), w_ffn2_out.astype(BF16)
    w_o = w_out.astype(BF16)

    c_all = jnp.zeros((8, d), F32).at[:n_batch].set(c).at[n_batch].set(c_ctx)
    mods = _ada_mod(c_all, w_ada, b_ada).reshape(depth, 8, N_MOD, d)

    h, hc = x, ctx
    for l in range(depth):
        last = l == depth - 1
        mod = [mods[l, :n_batch, k][:, None, :] for k in range(N_MOD)]
        modc = [mods[l, n_batch:n_batch + 1, k][:, None, :] for k in range(N_MOD)]
        h = _ffn(h, mod[0], mod[1], mod[2], w1i, w1o, l)
        hc = _ffn(hc, modc[0], modc[1], modc[2], w1i, w1o, l)
        col_major = (l % 2) == 1
        u = _prep(h, mod[3], mod[4])
        uc = _prep(hc, modc[3], modc[4])
        if col_major:
            u = u.reshape(n_batch, t_lat // GRID_W, GRID_W, d).transpose(0, 2, 1, 3).reshape(n_batch, t_lat, d)
        y, yc = _token_mixer(u, uc, not last, wts, l)
        if col_major:
            y = y.reshape(n_batch, GRID_W, t_lat // GRID_W, d).transpose(0, 2, 1, 3).reshape(n_batch * t_lat, d)
        h = _out_proj_resid(h, y, mod[5], w_o, l)
        h = _ffn(h, mod[6], mod[7], mod[8], w2i, w2o, l)
        if not last:
            hc = _out_proj_resid(hc, yc, modc[5], w_o, l)
            hc = _ffn(hc, modc[6], modc[7], modc[8], w2i, w2o, l)
    return _final_norm(h, final_gain)
```

```python
import functools
import math

import jax
import jax.numpy as jnp
from jax import lax
from jax.experimental import pallas as pl
from jax.experimental.pallas import tpu as pltpu

F32 = jnp.float32
BF16 = jnp.bfloat16
EPS = 1e-6
GRID_W = 64
N_MOD = 9
N_BRANCH = 4
S5_P = 16
S5_N = 64
S5_CHUNK = 16
LANES = 128
S5_LANE_GROUPS = LANES // S5_P
POOL_WINDOWS = (2, 4, 8, 16)
MLSTM_HEADS = 4
MLSTM_CHUNK = 128
BF16_ROWS = 16
HALO = BF16_ROWS
ROW_TILE = 1024
VMEM_LIMIT = 56 * 1024 * 1024
HI = lax.Precision.HIGHEST


def _params(n_grid_dims, fuse_inputs=None):
    return pltpu.CompilerParams(dimension_semantics=("arbitrary",) * n_grid_dims,
                                vmem_limit_bytes=VMEM_LIMIT, allow_input_fusion=fuse_inputs)


def _gelu_tanh(y):
    return 0.5 * y * (1.0 + jnp.tanh(math.sqrt(2.0 / math.pi) * (y + 0.044715 * (y * y * y))))


def _log_sigmoid(x):
    return jnp.minimum(x, 0.0) - jnp.log1p(jnp.exp(-jnp.abs(x)))


def _ada_kernel(c_ref, w_ref, b_ref, o_ref):
    c = c_ref[...]
    a = (c * jax.nn.sigmoid(c)).astype(BF16)
    o_ref[0] = jnp.dot(a, w_ref[0].astype(BF16), preferred_element_type=F32) + b_ref[0]


def _ada_mod(c_all, w_ada, b_ada):
    n_layers, d, nd = w_ada.shape
    tn = 1024
    return pl.pallas_call(
        _ada_kernel,
        grid=(n_layers, nd // tn),
        in_specs=[pl.BlockSpec((8, d), lambda l, j: (0, 0)),
                  pl.BlockSpec((1, d, tn), lambda l, j: (l, 0, j)),
                  pl.BlockSpec((1, 1, tn), lambda l, j: (l, 0, j))],
        out_specs=pl.BlockSpec((1, 8, tn), lambda l, j: (l, 0, j)),
        out_shape=jax.ShapeDtypeStruct((n_layers, 8, nd), F32),
        compiler_params=_params(2), name="ada_mod",
    )(c_all, w_ada, b_ada.reshape(n_layers, 1, nd))


def _prep_kernel(h_ref, shift_ref, scale_ref, o_ref):
    x = h_ref[0]
    ms = jnp.mean(x * x, axis=-1, keepdims=True)
    o_ref[0] = (x * lax.rsqrt(ms + EPS) * (1.0 + scale_ref[0]) + shift_ref[0]).astype(o_ref.dtype)


def _prep(h, shift, scale):
    b, t, d = h.shape
    per_batch = shift.shape[0] > 1
    mod_spec = pl.BlockSpec((1, 1, d), lambda bi, i: (bi if per_batch else 0, 0, 0))
    tr = min(t, 512)
    spec = pl.BlockSpec((1, tr, d), lambda bi, i: (bi, i, 0))
    return pl.pallas_call(
        _prep_kernel, grid=(b, t // tr), in_specs=[spec, mod_spec, mod_spec], out_specs=spec,
        out_shape=jax.ShapeDtypeStruct((b, t, d), BF16),
        compiler_params=_params(2), name="prep",
    )(h, shift, scale)


def _final_kernel(h_ref, g_ref, o_ref):
    x = h_ref[0]
    ms = jnp.mean(x * x, axis=-1, keepdims=True)
    o_ref[0] = x * lax.rsqrt(ms + EPS) * g_ref[...]


def _final_norm(h, gain):
    b, t, d = h.shape
    tr = min(t, 512)
    spec = pl.BlockSpec((1, tr, d), lambda bi, i: (bi, i, 0))
    return pl.pallas_call(
        _final_kernel, grid=(b, t // tr),
        in_specs=[spec, pl.BlockSpec((1, d), lambda bi, i: (0, 0))], out_specs=spec,
        out_shape=jax.ShapeDtypeStruct((b, t, d), F32),
        compiler_params=_params(2), name="final_norm",
    )(h, gain.reshape(1, d))


_NT_DIMS = (((1,), (1,)), ((), ()))


def _mm_kernel(*refs, n_w, n_extra, epilogue, w_transposed):
    x_ref = refs[0]
    w_refs = refs[1:1 + n_w]
    extra = refs[1 + n_w:1 + n_w + n_extra]
    o_ref = refs[1 + n_w + n_extra]
    x = x_ref[...]
    if w_transposed:
        accs = [lax.dot_general(x, w[...], _NT_DIMS, preferred_element_type=F32) for w in w_refs]
    else:
        accs = [jnp.dot(x, w[...], preferred_element_type=F32) for w in w_refs]
    o_ref[...] = epilogue(accs, x, extra).astype(o_ref.dtype)


def _mm(x, tm, tn, n_cols, w_specs, w_arrays, extra_specs, extra_arrays, epilogue, out_dtype, name,
        w_transposed=False):
    m, k = x.shape
    return pl.pallas_call(
        functools.partial(_mm_kernel, n_w=len(w_arrays), n_extra=len(extra_arrays), epilogue=epilogue,
                          w_transposed=w_transposed),
        grid=(m // tm, n_cols // tn),
        in_specs=[pl.BlockSpec((tm, k), lambda i, j: (i, 0))] + list(w_specs) + list(extra_specs),
        out_specs=pl.BlockSpec((tm, tn), lambda i, j: (i, j)),
        out_shape=jax.ShapeDtypeStruct((m, n_cols), out_dtype),
        compiler_params=_params(2, [False] + [True] * len(w_arrays) + [False] * len(extra_arrays)), name=name,
    )(x, *w_arrays, *extra_arrays)


def _layer_w_spec(l, k, tn, col_block_offset=0):
    return pl.BlockSpec((None, k, tn), lambda i, j: (l, 0, col_block_offset + j))


def _layer_wt_spec(l, k, tn, row_block_offset=0):
    return pl.BlockSpec((None, tn, k), lambda i, j: (l, row_block_offset + j, 0))


def _gate_spec(gate, tm, rows_per_batch, tn):
    per_batch = gate.shape[0] > 1
    return pl.BlockSpec((1, 1, tn), lambda i, j: ((i * tm) // rows_per_batch if per_batch else 0, 0, j))


def _swiglu_epilogue(accs, x, extra):
    a, g = accs
    return g * jax.nn.sigmoid(g) * a


def _resid_epilogue(accs, x, extra, *, coef):
    h_ref, gate_ref = extra
    return h_ref[...] + (coef * gate_ref[0]) * accs[0]


def _plain_epilogue(accs, x, extra):
    return accs[0]


def _bias_epilogue(accs, x, extra):
    return accs[0] + extra[0][...]


def _glu_epilogue(accs, x, extra):
    return x.astype(F32) * jax.nn.sigmoid(accs[0] + extra[0][...])


def _ffn(h, shift, scale, gate, w_in, w_out, l):
    b, t, d = h.shape
    m = b * t
    d_ff = w_out.shape[1]
    tm = min(m, ROW_TILE)
    tn = 512
    xn = _prep(h, shift, scale).reshape(m, d)
    act = _mm(xn, tm, tn, d_ff,
              [_layer_w_spec(l, d, tn), _layer_w_spec(l, d, tn, d_ff // tn)], [w_in, w_in],
              [], [], _swiglu_epilogue, BF16, "ffn_in")
    out = _mm(act, tm, tn, d,
              [_layer_w_spec(l, d_ff, tn)], [w_out],
              [pl.BlockSpec((tm, tn), lambda i, j: (i, j)), _gate_spec(gate, tm, t, tn)],
              [h.reshape(m, d), gate],
              functools.partial(_resid_epilogue, coef=0.5), F32, "ffn_out")
    return out.reshape(b, t, d)


def _merge_kernel(u_ref, *refs):
    b_refs = refs[:N_BRANCH]
    wg_refs = refs[N_BRANCH:2 * N_BRANCH]
    wb_ref, o_ref = refs[2 * N_BRANCH], refs[2 * N_BRANCH + 1]
    u = u_ref[...]
    acc = None
    for k in range(N_BRANCH):
        gate = jax.nn.sigmoid(lax.dot_general(u, wg_refs[k][0], _NT_DIMS, preferred_element_type=F32))
        val = gate * jnp.dot(b_refs[k][...], wb_ref[k], preferred_element_type=F32)
        acc = val if acc is None else acc + val
    o_ref[...] = acc.astype(o_ref.dtype)


def _merge(u, branches, w_in_t, gate_row0, w_branch, l):
    m, d = u.shape
    w = branches[0].shape[1]
    tm = min(m, ROW_TILE)
    tn = 256
    assert gate_row0 % BF16_ROWS == 0, "gate rows must start on a packed sublane tile"
    br_spec = pl.BlockSpec((tm, w), lambda i, j: (i, 0), pipeline_mode=pl.Buffered(1))

    def gate_spec(k):
        return pl.BlockSpec((pl.Element(1), pl.Element(tn), pl.Element(d)),
                            lambda i, j: (l, pl.multiple_of(gate_row0 + k * d + j * tn, BF16_ROWS), 0))

    return pl.pallas_call(
        _merge_kernel, grid=(m // tm, d // tn),
        in_specs=[pl.BlockSpec((tm, d), lambda i, j: (i, 0))] + [br_spec] * N_BRANCH
                 + [gate_spec(k) for k in range(N_BRANCH)]
                 + [pl.BlockSpec((None, N_BRANCH, w, tn), lambda i, j: (l, 0, 0, j))],
        out_specs=pl.BlockSpec((tm, tn), lambda i, j: (i, j)),
        out_shape=jax.ShapeDtypeStruct((m, d), BF16),
        compiler_params=_params(2), name="merge",
    )(u, *branches, *([w_in_t] * N_BRANCH), w_branch)


def _s5_factors(a_re, a_im, log_dt, b_re, b_im, c_re, c_im):
    a_re, a_im = a_re.astype(F32), a_im.astype(F32)
    n_groups = a_re.shape[1]
    lc, p, n, gl = S5_CHUNK, S5_P, S5_N, S5_LANE_GROUPS
    nb = n_groups // gl
    dt = jnp.exp(log_dt.astype(F32))[:, :, None]
    ks = jnp.arange(lc + 1, dtype=F32)[:, None, None, None]
    mag = jnp.exp(ks * (dt * a_re))
    pw = jnp.stack([mag * jnp.cos(ks * (dt * a_im)), mag * jnp.sin(ks * (dt * a_im))])
    lam_re, lam_im = pw[0, 1], pw[1, 1]
    den = a_re * a_re + a_im * a_im
    z_re = ((lam_re - 1.0) * a_re + lam_im * a_im) / den
    z_im = (lam_im * a_re - (lam_re - 1.0) * a_im) / den
    b_re, b_im = b_re.astype(F32), b_im.astype(F32)
    bb = jnp.stack([z_re[..., None] * b_re - z_im[..., None] * b_im,
                    z_re[..., None] * b_im + z_im[..., None] * b_re])
    cc = jnp.stack([c_re.astype(F32), c_im.astype(F32)])
    same = jnp.eye(gl, dtype=F32)
    cb = cc.reshape(2, 2, nb, gl, p, n).transpose(2, 1, 0, 3, 5, 4)
    cb = (cb[:, :, :, :, :, None, :] * same[None, None, None, :, None, :, None]).reshape(nb, 2, 2, gl * n, gl * p)
    bd = bb.reshape(2, 2, nb, gl, n, p).transpose(2, 1, 0, 3, 5, 4)
    bd = (bd[:, :, :, :, :, None, :] * same[None, None, None, :, None, :, None]).reshape(nb, 2, 2, gl * p, gl * n)
    pwr = pw.reshape(2, lc + 1, 2, nb, gl * n).transpose(3, 2, 0, 1, 4)
    pwc = pwr.transpose(0, 1, 2, 4, 3)
    return cb, bd, pwr, pwc


def _s5_tables_kernel(cb_ref, bd_ref, pwr_ref, pwc_ref, skip_ref, toep_ref, ec_ref, bc_ref, k_ref):
    s = pl.program_id(1)
    lc = S5_CHUNK
    ncol = pwc_ref.shape[3]

    def column(d, comp, e):
        lane = lax.broadcasted_iota(jnp.int32, (ncol, pwc_ref.shape[4]), 1)
        return jnp.sum(jnp.where(lane == e, pwc_ref[0, d, comp], 0.0), axis=1, keepdims=True)

    def c_times_power(d, e):
        pr, pi = column(d, 0, e), column(d, 1, e)
        return cb_ref[0, d, 0] * pr - cb_ref[0, d, 1] * pi, cb_ref[0, d, 0] * pi + cb_ref[0, d, 1] * pr

    @pl.when(s == 0)
    def _():
        for d in range(2):
            x_cat = jnp.concatenate([bd_ref[0, d, 0], -bd_ref[0, d, 1]], axis=1)
            for tau in range(lc):
                yr, yi = c_times_power(d, tau)
                k_ref[d, tau] = jnp.dot(x_cat, jnp.concatenate([yr, yi], axis=0), precision=HI,
                                        preferred_element_type=F32)

    rows = lax.broadcasted_iota(jnp.int32, (LANES, LANES), 0)
    cols = lax.broadcasted_iota(jnp.int32, (LANES, LANES), 1)
    diag = jnp.where(rows == cols, skip_ref[0], 0.0)
    for r in range(lc):
        blk = (jnp.where(s >= r, k_ref[0, jnp.maximum(s - r, 0)], 0.0)
               + jnp.where(s <= r, k_ref[1, jnp.maximum(r - s, 0)], 0.0)
               + jnp.where(s == r, diag, 0.0))
        toep_ref[0, r * LANES:(r + 1) * LANES, :] = blk.astype(toep_ref.dtype)
    for d in range(2):
        yr, yi = c_times_power(d, (s + 1) if d == 0 else (lc - s))
        ec_ref[0, d * ncol:(d + 1) * ncol, :] = yr.astype(ec_ref.dtype)
        ec_ref[0, (2 + d) * ncol:(3 + d) * ncol, :] = (-yi).astype(ec_ref.dtype)
        f = (lc - 1 - s) if d == 0 else s
        fr, fi = pwr_ref[0, d, 0, pl.ds(f, 1), :], pwr_ref[0, d, 1, pl.ds(f, 1), :]
        bc_ref[0, :, d * ncol:(d + 1) * ncol] = (bd_ref[0, d, 0] * fr - bd_ref[0, d, 1] * fi).astype(bc_ref.dtype)
        bc_ref[0, :, (2 + d) * ncol:(3 + d) * ncol] = (bd_ref[0, d, 0] * fi + bd_ref[0, d, 1] * fr).astype(bc_ref.dtype)


def _s5_all_tables(a_re, a_im, log_dt, b_re, b_im, c_re, c_im, d_skip):
    cb, bd, pwr, pwc = jax.vmap(_s5_factors)(a_re, a_im, log_dt, b_re, b_im, c_re, c_im)
    depth, nb = cb.shape[:2]
    lc = S5_CHUNK
    nk = depth * nb
    flat = lambda t: t.reshape((nk,) + t.shape[2:])
    cb, bd, pwr, pwc = flat(cb), flat(bd), flat(pwr), flat(pwc)
    ncol, nrow = cb.shape[3], cb.shape[4]
    skip = d_skip.astype(F32).reshape(nk, 1, nrow)
    whole = lambda t: pl.BlockSpec((1,) + t.shape[1:], lambda k, s: (k,) + (0,) * (t.ndim - 1))
    toep, ec, bc = pl.pallas_call(
        _s5_tables_kernel, grid=(nk, lc),
        in_specs=[whole(cb), whole(bd), whole(pwr), whole(pwc), whole(skip)],
        out_specs=[pl.BlockSpec((1, lc * nrow, nrow), lambda k, s: (k, 0, s)),
                   pl.BlockSpec((1, 4 * ncol, nrow), lambda k, s: (k, 0, s)),
                   pl.BlockSpec((1, nrow, 4 * ncol), lambda k, s: (k, s, 0))],
        out_shape=[jax.ShapeDtypeStruct((nk, lc * nrow, lc * nrow), BF16),
                   jax.ShapeDtypeStruct((nk, 4 * ncol, lc * nrow), BF16),
                   jax.ShapeDtypeStruct((nk, lc * nrow, 4 * ncol), BF16)],
        scratch_shapes=[pltpu.VMEM((2, lc, nrow, nrow), F32)],
        compiler_params=_params(2), name="s5_tables",
    )(cb, bd, pwr, pwc, skip)
    unflat = lambda t: t.reshape((depth, nb) + t.shape[1:])
    groups_states = nb * ncol
    lam_re = pwr[:, :, 0, lc].reshape(depth, nb, 2, ncol).transpose(0, 2, 1, 3).reshape(depth, 2, 1, groups_states)
    lam_im = pwr[:, :, 1, lc].reshape(depth, nb, 2, ncol).transpose(0, 2, 1, 3).reshape(depth, 2, 1, groups_states)
    return unflat(toep), unflat(ec), unflat(bc), lam_re, lam_im


def _s5_proj_kernel(x_ref, w_ref, o_ref, acc_ref):
    acc = lax.dot_general(x_ref[...], w_ref[...], _NT_DIMS, preferred_element_type=F32)
    n_rows = o_ref.shape[1]
    for blk in range(o_ref.shape[0]):
        acc_ref[blk] = acc[:, blk * LANES:(blk + 1) * LANES]
        for r in range(S5_CHUNK):
            rows = acc_ref[blk, pl.ds(r, n_rows, stride=S5_CHUNK), :]
            o_ref[blk, :, r * LANES:(r + 1) * LANES] = rows.astype(o_ref.dtype)


def _s5_proj(u2, w_main_t, l, w):
    m, d = u2.shape
    tm = min(m, ROW_TILE)
    nb = w // LANES
    return pl.pallas_call(
        _s5_proj_kernel, grid=(m // tm,),
        in_specs=[pl.BlockSpec((tm, d), lambda i: (i, 0)), pl.BlockSpec((None, w, d), lambda i: (l, 0, 0))],
        out_specs=pl.BlockSpec((nb, tm // S5_CHUNK, S5_CHUNK * LANES), lambda i: (0, i, 0)),
        out_shape=jax.ShapeDtypeStruct((nb, m // S5_CHUNK, S5_CHUNK * LANES), BF16),
        scratch_shapes=[pltpu.VMEM((nb, tm, LANES), F32)],
        compiler_params=_params(1), name="s5_proj",
    )(u2, w_main_t)


def _s5_in_kernel(u_ref, w_ref, re_ref, im_ref):
    acc = jnp.dot(u_ref[0], w_ref[0], preferred_element_type=F32)
    w = re_ref.shape[2]
    re_ref[0] = acc[:, 0:w]
    re_ref[1] = acc[:, w:2 * w]
    im_ref[0] = acc[:, 2 * w:3 * w]
    im_ref[1] = acc[:, 3 * w:4 * w]


def _s5_scan_kernel(lre, lim, cre, cim, lr_ref, li_ref, o_lre, o_lim, o_cre, o_cim, *, n_batch, n_lat, n_ctx):
    d = pl.program_id(0)
    lr = lr_ref[0]
    li = li_ref[0]

    def run(re_ref, im_ref, ore_ref, oim_ref, n_chunks, carry):
        def body(j, carry):
            idx = jnp.where(d == 0, j, n_chunks - 1 - j)
            new = []
            for bi in range(n_batch):
                xr, xi = carry[bi]
                row = bi * n_chunks + idx
                ore_ref[0, pl.ds(row, 1), :] = xr
                oim_ref[0, pl.ds(row, 1), :] = xi
                ur = re_ref[0, pl.ds(row, 1), :]
                ui = im_ref[0, pl.ds(row, 1), :]
                new.append((lr * xr - li * xi + ur, lr * xi + li * xr + ui))
            return tuple(new)

        return lax.fori_loop(0, n_chunks, body, carry)

    zero = jnp.zeros(lr.shape, F32)
    carry = run(cre, cim, o_cre, o_cim, n_ctx, tuple((zero, zero) for _ in range(n_batch)))
    run(lre, lim, o_lre, o_lim, n_lat, carry)


def _s5_out_kernel(u_ref, xr_ref, xi_ref, t_ref, e_ref, o_ref):
    xcat = jnp.concatenate([xr_ref[0], xr_ref[1], xi_ref[0], xi_ref[1]], axis=1).astype(BF16)
    y = (jnp.dot(u_ref[0], t_ref[0], preferred_element_type=F32)
         + jnp.dot(xcat, e_ref[0], preferred_element_type=F32))
    o_ref[0] = _gelu_tanh(y).astype(o_ref.dtype)


def _s5_glu_kernel(y_ref, w_ref, b_ref, o_ref, g_ref):
    n_rows = y_ref.shape[1]
    for blk in range(y_ref.shape[0]):
        for s in range(S5_CHUNK):
            g_ref[blk, pl.ds(s, n_rows, stride=S5_CHUNK), :] = y_ref[blk, :, s * LANES:(s + 1) * LANES].astype(F32)
    g = jnp.concatenate([g_ref[blk] for blk in range(y_ref.shape[0])], axis=1)
    z = jnp.dot(g.astype(BF16), w_ref[...], preferred_element_type=F32) + b_ref[...]
    o_ref[...] = (g * jax.nn.sigmoid(z)).astype(o_ref.dtype)


def _s5_branch(uc_lat, uc_ctx, n_batch, tables, w_glu, b_glu, l, want_ctx):
    toep, ec, bc, lam_re, lam_im = tables
    nb, r_lat, cw = uc_lat.shape
    r_ctx = uc_ctx.shape[1]
    gn = lam_re.shape[3]
    sw = gn // nb
    w = nb * LANES

    def chunk_rows(r):
        return min(r, 512)

    def state_in(uc):
        r = uc.shape[1]
        rt = chunk_rows(r)
        x_spec = pl.BlockSpec((2, rt, sw), lambda k, i: (0, i, k))
        x_shape = jax.ShapeDtypeStruct((2, r, gn), F32)
        return pl.pallas_call(
            _s5_in_kernel, grid=(nb, r // rt),
            in_specs=[pl.BlockSpec((1, rt, cw), lambda k, i: (k, i, 0)),
                      pl.BlockSpec((None, 1, cw, 4 * sw), lambda k, i: (l, k, 0, 0))],
            out_specs=[x_spec, x_spec], out_shape=[x_shape, x_shape],
            compiler_params=_params(2), name="s5_in",
        )(uc, bc)

    lre, lim = state_in(uc_lat)
    cre, cim = state_in(uc_ctx)
    lb = 512
    lat_blk = pl.BlockSpec((1, r_lat, lb), lambda d, j: (d, 0, j))
    ctx_blk = pl.BlockSpec((1, r_ctx, lb), lambda d, j: (d, 0, j))
    lam_spec = pl.BlockSpec((None, 1, 1, lb), lambda d, j: (l, d, 0, j))
    lat_shape = jax.ShapeDtypeStruct((2, r_lat, gn), F32)
    ctx_shape = jax.ShapeDtypeStruct((2, r_ctx, gn), F32)
    xl_re, xl_im, xc_re, xc_im = pl.pallas_call(
        functools.partial(_s5_scan_kernel, n_batch=n_batch, n_lat=r_lat // n_batch, n_ctx=r_ctx // n_batch),
        grid=(2, gn // lb),
        in_specs=[lat_blk, lat_blk, ctx_blk, ctx_blk, lam_spec, lam_spec],
        out_specs=[lat_blk, lat_blk, ctx_blk, ctx_blk],
        out_shape=[lat_shape, lat_shape, ctx_shape, ctx_shape],
        compiler_params=_params(2), name="s5_scan",
    )(lre, lim, cre, cim, lam_re, lam_im)

    def readout(uc, x_re, x_im):
        r = uc.shape[1]
        rt = chunk_rows(r)
        hw = cw // 2
        x_spec = pl.BlockSpec((2, rt, sw), lambda k, h, i: (0, i, k))
        return pl.pallas_call(
            _s5_out_kernel, grid=(nb, 2, r // rt),
            in_specs=[pl.BlockSpec((1, rt, cw), lambda k, h, i: (k, i, 0)), x_spec, x_spec,
                      pl.BlockSpec((None, 1, cw, hw), lambda k, h, i: (l, k, 0, h)),
                      pl.BlockSpec((None, 1, 4 * sw, hw), lambda k, h, i: (l, k, 0, h))],
            out_specs=pl.BlockSpec((1, rt, hw), lambda k, h, i: (k, i, h)),
            out_shape=jax.ShapeDtypeStruct((nb, r, cw), BF16),
            compiler_params=_params(3), name="s5_out",
        )(uc, x_re, x_im, toep, ec)

    def glu(y):
        m = y.shape[1] * S5_CHUNK
        tm = min(m, ROW_TILE)
        return pl.pallas_call(
            _s5_glu_kernel, grid=(m // tm,),
            in_specs=[pl.BlockSpec((nb, tm // S5_CHUNK, cw), lambda i: (0, i, 0)),
                      pl.BlockSpec((None, w, w), lambda i: (l, 0, 0)),
                      pl.BlockSpec((None, 1, w), lambda i: (l, 0, 0))],
            out_specs=pl.BlockSpec((tm, w), lambda i: (i, 0)),
            out_shape=jax.ShapeDtypeStruct((m, w), BF16),
            scratch_shapes=[pltpu.VMEM((nb, tm, LANES), F32)],
            compiler_params=_params(1), name="s5_glu",
        )(y, w_glu, b_glu)

    out_lat = glu(readout(uc_lat, xl_re, xl_im))
    out_ctx = glu(readout(uc_ctx, xc_re, xc_im)) if want_ctx else None
    return out_lat, out_ctx


def _conv_kernel(hm, hp, hn, cm, cp, cn, bm, w_ref, o_ref, *, n_tiles):
    i = pl.program_id(1)
    xg = hm[0].astype(F32) * cm[0].astype(F32)
    tm = xg.shape[0]
    x_before = jnp.where(i > 0, hp[0, HALO - 1:HALO].astype(F32) * cp[0, HALO - 1:HALO].astype(F32), 0.0)
    x_after = jnp.where(i < n_tiles - 1, hn[0, 0:1].astype(F32) * cn[0, 0:1].astype(F32), 0.0)
    row = lax.broadcasted_iota(jnp.int32, xg.shape, 0)
    prev = jnp.where(row == 0, x_before, pltpu.roll(xg, 1, axis=0))
    nxt = jnp.where(row == tm - 1, x_after, pltpu.roll(xg, tm - 1, axis=0))
    w = w_ref[...]
    y = w[0:1] * prev + w[1:2] * xg + w[2:3] * nxt
    o_ref[0] = (bm[0].astype(F32) * y).astype(o_ref.dtype)


def _halo_specs(tm, width, col_block, n_halo_blocks):
    hb = tm // HALO
    main = pl.BlockSpec((1, tm, width), lambda b, i, c: (b, i, col_block + c))
    before = pl.BlockSpec((1, HALO, width), lambda b, i, c: (b, jnp.maximum(i * hb - 1, 0), col_block + c))
    after = pl.BlockSpec((1, HALO, width),
                         lambda b, i, c: (b, jnp.minimum((i + 1) * hb, n_halo_blocks - 1), col_block + c))
    return main, before, after


def _conv_branch(z, conv_w, l, col0):
    b, t, _ = z.shape
    w = conv_w.shape[2]
    tm = min(t, ROW_TILE)
    tc = 512
    n_tiles = t // tm
    cb = col0 // tc
    h_specs = _halo_specs(tm, tc, cb, t // HALO)
    bg_spec = _halo_specs(tm, tc, cb + w // tc, t // HALO)[0]
    c_specs = _halo_specs(tm, tc, cb + 2 * (w // tc), t // HALO)
    out = pl.pallas_call(
        functools.partial(_conv_kernel, n_tiles=n_tiles), grid=(b, n_tiles, w // tc),
        in_specs=[*h_specs, *c_specs, bg_spec, pl.BlockSpec((None, 3, tc), lambda bi, i, c: (l, 0, c))],
        out_specs=pl.BlockSpec((1, tm, tc), lambda bi, i, c: (bi, i, c)),
        out_shape=jax.ShapeDtypeStruct((b, t, w), BF16),
        compiler_params=_params(3), name="conv",
    )(z, z, z, z, z, z, z, conv_w)
    return out.reshape(b * t, w)


def _pool_kernel(um, up, un, w_ref, s_ref, o_ref, *, n_tiles, seq_len):
    i = pl.program_id(1)
    gi = pl.program_id(2)
    xm = um[0]
    tm = xm.shape[0]
    win = jnp.left_shift(POOL_WINDOWS[0], gi)
    half = win // 2
    x_before = jnp.where(i > 0, up[0], jnp.zeros_like(up[0]))
    x_after = jnp.where(i < n_tiles - 1, un[0], jnp.zeros_like(un[0]))
    ext = jnp.concatenate([x_before, xm, x_after], axis=0)
    s = lax.broadcasted_iota(jnp.int32, (tm, tm + 2 * HALO), 0)
    r = lax.broadcasted_iota(jnp.int32, (tm, tm + 2 * HALO), 1) - HALO
    off = r - s
    band = jnp.where(off >= -half, jnp.where(off < win - half, 1.0, 0.0), 0.0).astype(BF16)
    wsum = jnp.dot(band, ext, preferred_element_type=F32)
    t = i * tm + lax.broadcasted_iota(jnp.int32, (tm, 1), 0)
    cnt = (jnp.minimum(t + win - half, seq_len) - jnp.maximum(t - half, 0)).astype(F32)
    p = wsum / cnt - xm.astype(F32)
    y = jnp.dot(p.astype(BF16), w_ref[...], preferred_element_type=F32) * s_ref[...]
    o_ref[0] = y.astype(o_ref.dtype)


def _pool_branch(z, pool_w, pool_scale, l, col0):
    b, t, _ = z.shape
    n_groups, gw = pool_w.shape[1], pool_w.shape[2]
    tm = min(t, 512)
    n_tiles = t // tm
    specs = _halo_specs(tm, gw, col0 // gw, t // HALO)
    out = pl.pallas_call(
        functools.partial(_pool_kernel, n_tiles=n_tiles, seq_len=t), grid=(b, n_tiles, n_groups),
        in_specs=[*specs, pl.BlockSpec((None, None, gw, gw), lambda bi, i, c: (l, c, 0, 0)),
                  pl.BlockSpec((None, 1, gw), lambda bi, i, c: (l, 0, c))],
        out_specs=pl.BlockSpec((1, tm, gw), lambda bi, i, c: (bi, i, c)),
        out_shape=jax.ShapeDtypeStruct((b, t, n_groups * gw), BF16),
        compiler_params=_params(3), name="pool",
    )(z, z, z, pool_w, pool_scale)
    return out.reshape(b * t, n_groups * gw)


def _mlstm_chunk(q, k, v, li_col, lf_col, li_row, lf_row, state, reverse):
    c_mat, n_row, m = state
    lc = q.shape[0]
    ti = lax.broadcasted_iota(jnp.int32, (lc, lc), 0)
    si = lax.broadcasted_iota(jnp.int32, (lc, lc), 1)
    seen = (si >= ti) if reverse else (si <= ti)
    seen_t = (ti >= si) if reverse else (ti <= si)
    bcum_col = jnp.sum(jnp.where(seen, lf_row, 0.0), axis=1, keepdims=True)
    bcum_row = jnp.sum(jnp.where(seen_t, lf_col, 0.0), axis=0, keepdims=True)
    b_last = jnp.sum(lf_row, axis=1, keepdims=True)
    dmat = bcum_col - bcum_row + li_row
    m_inter = bcum_col + m
    m_t = jnp.maximum(jnp.max(jnp.where(seen, dmat, -1e30), axis=1, keepdims=True), m_inter)
    scores = lax.dot_general(q, k, (((1,), (1,)), ((), ())), preferred_element_type=F32)
    wgt = jnp.where(seen, scores * jnp.exp(dmat - m_t), 0.0)
    decay = jnp.exp(m_inter - m_t)
    num = (jnp.dot(wgt.astype(BF16), v, preferred_element_type=F32)
           + decay * jnp.dot(q, c_mat.astype(BF16), preferred_element_type=F32))
    den = (jnp.sum(wgt, axis=1, keepdims=True)
           + decay * jnp.sum(q.astype(F32) * n_row, axis=1, keepdims=True))
    h = num / jnp.maximum(jnp.abs(den), jnp.exp(-m_t))
    g_col = b_last - bcum_col + li_col
    m_new = jnp.maximum(b_last + m, jnp.max(g_col, axis=0, keepdims=True))
    carry = jnp.exp(b_last + m - m_new)
    wk = k.astype(F32) * jnp.exp(g_col - m_new)
    c_new = carry * c_mat + jnp.dot(wk.T.astype(BF16), v, preferred_element_type=F32)
    n_new = carry * n_row + jnp.sum(wk, axis=0, keepdims=True)
    return h, (c_new, n_new, m_new)


def _mlstm_kernel(ql, kl, vl, ol, gcl, grl, qc, kc, vc, oc, gcc, grc, gain_ref, out_l, out_c, h_l, h_c):
    lc = MLSTM_CHUNK
    dk, dv = ql.shape[2], vl.shape[2]

    def run(refs, h_ref, n_chunks, states):
        q_ref, k_ref, v_ref, gc_ref, gr_ref = refs

        def one(c, st, d):
            t0 = pl.multiple_of(c * lc, lc)
            gcol = gc_ref[0, 0, pl.ds(t0, lc), :]
            li_col = gcol[:, d:d + 1]
            lf_col = _log_sigmoid(gcol[:, 2 + d:3 + d])
            li_row = gr_ref[0, 0, d, pl.ds(c, 1), :]
            lf_row = _log_sigmoid(gr_ref[0, 0, 2 + d, pl.ds(c, 1), :])
            h, st = _mlstm_chunk(q_ref[0, pl.ds(t0, lc), :], k_ref[0, pl.ds(t0, lc), :],
                                 v_ref[0, pl.ds(t0, lc), :], li_col, lf_col, li_row, lf_row, st, d == 1)
            return t0, h, st

        def make_body(second_visit):
            def body(j, sts):
                t_f, h_f, st_f = one(j, sts[0], 0)
                t_b, h_b, st_b = one(n_chunks - 1 - j, sts[1], 1)
                for t0, h in ((t_f, h_f), (t_b, h_b)):
                    if second_visit:
                        h_ref[pl.ds(t0, lc), :] += h
                    else:
                        h_ref[pl.ds(t0, lc), :] = h
                return (st_f, st_b)
            return body

        half = n_chunks // 2
        states = lax.fori_loop(0, half, make_body(False), states)
        return lax.fori_loop(half, n_chunks, make_body(True), states)

    lat = (ql, kl, vl, gcl, grl)
    ctx = (qc, kc, vc, gcc, grc)
    n_lat, n_ctx = ql.shape[1] // lc, qc.shape[1] // lc
    zero = (jnp.zeros((dk, dv), F32), jnp.zeros((1, dk), F32), jnp.zeros((1, 1), F32))
    states = run(ctx, h_c, n_ctx, (zero, zero))
    run(lat, h_l, n_lat, states)

    gain = gain_ref[...]

    def readout(h_ref, o_ref, out_ref):
        t = h_ref.shape[0]
        tile = min(t, 512)

        def body(i, _):
            t0 = pl.multiple_of(i * tile, tile)
            h = h_ref[pl.ds(t0, tile), :]
            hn = h * lax.rsqrt(jnp.mean(h * h, axis=-1, keepdims=True) + EPS) * gain
            og = jax.nn.sigmoid(o_ref[0, pl.ds(t0, tile), :].astype(F32))
            out_ref[0, pl.ds(t0, tile), :] = (hn * og).astype(out_ref.dtype)
            return 0

        lax.fori_loop(0, t // tile, body, 0)

    readout(h_l, ol, out_l)
    readout(h_c, oc, out_c)


def _mlstm_branch(z_lat, zg_lat, z_ctx, zg_ctx, norm_gain, l, col_q, col_k, col_v, col_o):
    b, t, _ = z_lat.shape
    tc = z_ctx.shape[1]
    nh = MLSTM_HEADS
    w = norm_gain.shape[2]
    dv = w // nh
    dk = dv // 2
    lc = MLSTM_CHUNK
    assert (t // lc) % 2 == 0 and (tc // lc) % 2 == 0, "the two-direction loop pairs chunks"

    def gate_layouts(zg, tt):
        g = zg[:, :4 * nh].reshape(b, tt, 4, nh)
        col = g.transpose(0, 3, 1, 2)
        row = g.transpose(0, 3, 2, 1).reshape(b, nh, 4, tt // lc, lc)
        return col, row

    gcl, grl = gate_layouts(zg_lat, t)
    gcc, grc = gate_layouts(zg_ctx, tc)
    once = pl.Buffered(1)

    def specs(tt):
        return [pl.BlockSpec((1, tt, dk), lambda bi, h: (bi, 0, col_q // dk + h), pipeline_mode=once),
                pl.BlockSpec((1, tt, dk), lambda bi, h: (bi, 0, col_k // dk + h), pipeline_mode=once),
                pl.BlockSpec((1, tt, dv), lambda bi, h: (bi, 0, col_v // dv + h), pipeline_mode=once),
                pl.BlockSpec((1, tt, dv), lambda bi, h: (bi, 0, col_o // dv + h), pipeline_mode=once),
                pl.BlockSpec((1, 1, tt, 4), lambda bi, h: (bi, h, 0, 0), pipeline_mode=once),
                pl.BlockSpec((1, 1, 4, tt // lc, lc), lambda bi, h: (bi, h, 0, 0, 0))]

    out_l, out_c = pl.pallas_call(
        _mlstm_kernel, grid=(b, nh),
        in_specs=specs(t) + specs(tc) + [pl.BlockSpec((None, 1, dv), lambda bi, h: (l, 0, h))],
        out_specs=[pl.BlockSpec((1, t, dv), lambda bi, h: (bi, 0, h)),
                   pl.BlockSpec((1, tc, dv), lambda bi, h: (bi, 0, h))],
        out_shape=[jax.ShapeDtypeStruct((b, t, w), BF16), jax.ShapeDtypeStruct((b, tc, w), BF16)],
        scratch_shapes=[pltpu.VMEM((t, dv), F32), pltpu.VMEM((tc, dv), F32)],
        compiler_params=_params(2), name="mlstm",
    )(z_lat, z_lat, z_lat, z_lat, gcl, grl, z_ctx, z_ctx, z_ctx, z_ctx, gcc, grc, norm_gain)
    return out_l.reshape(b * t, w), out_c.reshape(b * tc, w)


def _token_mixer(u_lat, u_ctx, want_ctx, wts, l):
    b, t, d = u_lat.shape
    tc = u_ctx.shape[1]
    w = d // 4
    n_rest = 7 * w
    col_conv, col_pool, col_q, col_k, col_v, col_o = 0, 3 * w, 4 * w, 4 * w + w // 2, 5 * w, 6 * w

    def in_proj(u):
        m = u.shape[0] * u.shape[1]
        u2 = u.reshape(m, d)
        tm = min(m, ROW_TILE)
        us5 = _s5_proj(u2, wts['w_in_t'], l, w)
        z = _mm(u2, tm, w, n_rest, [_layer_wt_spec(l, d, w, 1)], [wts['w_in_t']], [], [],
                _plain_epilogue, BF16, "in_proj", w_transposed=True)
        zg = _mm(u2, tm, LANES, LANES, [_layer_wt_spec(l, d, LANES)], [wts['w_mgate_t']],
                 [pl.BlockSpec((None, 1, LANES), lambda i, j: (l, 0, 0))], [wts['mgate_bias']],
                 _bias_epilogue, F32, "in_proj_gates", w_transposed=True)
        return u2, us5, z.reshape(u.shape[0], u.shape[1], n_rest), zg

    u2_lat, us5_lat, z_lat, zg_lat = in_proj(u_lat)
    u2_ctx, us5_ctx, z_ctx, zg_ctx = in_proj(u_ctx)
    s5_l, s5_c = _s5_branch(us5_lat, us5_ctx, b, wts['s5_tables'],
                            wts['s5_w_glu'], wts['s5_b_glu'], l, want_ctx)
    ml_l, ml_c = _mlstm_branch(z_lat, zg_lat, z_ctx, zg_ctx, wts['mlstm_norm_gain'], l,
                               col_q, col_k, col_v, col_o)
    conv_l = _conv_branch(z_lat, wts['conv_w'], l, col_conv)
    pool_l = _pool_branch(z_lat, wts['pool_w'], wts['pool_scale'], l, col_pool)
    y_lat = _merge(u2_lat, (s5_l, conv_l, pool_l, ml_l), wts['w_in_t'], wts['gate_row0'], wts['w_branch'], l)
    if not want_ctx:
        return y_lat, None
    conv_c = _conv_branch(z_ctx, wts['conv_w'], l, col_conv)
    pool_c = _pool_branch(z_ctx, wts['pool_w'], wts['pool_scale'], l, col_pool)
    y_ctx = _merge(u2_ctx, (s5_c, conv_c, pool_c, ml_c), wts['w_in_t'], wts['gate_row0'], wts['w_branch'], l)
    return y_lat, y_ctx


def _out_proj_resid(h, y, gate, w_out, l):
    b, t, d = h.shape
    m = b * t
    tm = min(m, ROW_TILE)
    tn = 512
    out = _mm(y, tm, tn, d, [_layer_w_spec(l, d, tn)], [w_out],
              [pl.BlockSpec((tm, tn), lambda i, j: (i, j)), _gate_spec(gate, tm, t, tn)],
              [h.reshape(m, d), gate], functools.partial(_resid_epilogue, coef=1.0), F32, "out_proj_resid")
    return out.reshape(b, t, d)


def kernel(x, c, ctx, c_ctx, w_ada, b_ada, w_ffn1_in, w_ffn1_out, w_ffn2_in, w_ffn2_out, w_in, s5_a_re, s5_a_im, s5_log_dt, s5_b_re, s5_b_im, s5_c_re, s5_c_im, s5_d, s5_w_glu, s5_b_glu, conv_w, pool_w, pool_scale, mlstm_gate_bias, mlstm_norm_gain, w_branch, w_out, final_gain):
    n_batch, t_lat, d = x.shape
    depth = w_ada.shape[0]
    w = d // 4
    n_main = 8 * w
    n_mgate = 4 * MLSTM_HEADS
    dk = w // MLSTM_HEADS // 2

    row_scale = jnp.ones((w_in.shape[2], 1), F32).at[5 * w:5 * w + w // 2].set(dk ** -0.5)
    w_in_t = (jnp.transpose(w_in, (0, 2, 1)) * row_scale).astype(BF16)
    wts = {
        'w_in_t': w_in_t,
        'gate_row0': n_main + n_mgate,
        'w_mgate_t': jnp.pad(w_in_t[:, n_main:n_main + n_mgate], ((0, 0), (0, LANES - n_mgate), (0, 0))),
        'mgate_bias': jnp.pad(mlstm_gate_bias.reshape(depth, 1, n_mgate).astype(F32),
                              ((0, 0), (0, 0), (0, LANES - n_mgate))),
        'w_branch': w_branch.astype(BF16),
        's5_w_glu': s5_w_glu.astype(BF16),
        's5_b_glu': s5_b_glu.astype(F32).reshape(depth, 1, w),
        'conv_w': conv_w.astype(F32),
        'pool_w': pool_w.astype(BF16),
        'pool_scale': pool_scale.astype(F32).reshape(depth, 1, w),
        'mlstm_norm_gain': mlstm_norm_gain.astype(F32).reshape(depth, 1, w),
        's5_tables': _s5_all_tables(s5_a_re, s5_a_im, s5_log_dt, s5_b_re, s5_b_im, s5_c_re, s5_c_im, s5_d),
    }
    w1i, w1o = w_ffn1_in.astype(BF16), w_ffn1_out.astype(BF16)
    w2i, w2o = w_ffn2_in.astype(BF16), w_ffn2_out.astype(BF16)
    w_o = w_out.astype(BF16)

    c_all = jnp.zeros((8, d), F32).at[:n_batch].set(c).at[n_batch].set(c_ctx)
    mods = _ada_mod(c_all, w_ada, b_ada).reshape(depth, 8, N_MOD, d)

    h, hc = x, ctx
    for l in range(depth):
        last = l == depth - 1
        mod = [mods[l, :n_batch, k][:, None, :] for k in range(N_MOD)]
        modc = [mods[l, n_batch:n_batch + 1, k][:, None, :] for k in range(N_MOD)]
        h = _ffn(h, mod[0], mod[1], mod[2], w1i, w1o, l)
        hc = _ffn(hc, modc[0], modc[1], modc[2], w1i, w1o, l)
        col_major = (l % 2) == 1
        u = _prep(h, mod[3], mod[4])
        uc = _prep(hc, modc[3], modc[4])
        if col_major:
            u = u.reshape(n_batch, t_lat // GRID_W, GRID_W, d).transpose(0, 2, 1, 3).reshape(n_batch, t_lat, d)
        y, yc = _token_mixer(u, uc, not last, wts, l)
        if col_major:
            y = y.reshape(n_batch, GRID_W, t_lat // GRID_W, d).transpose(0, 2, 1, 3).reshape(n_batch * t_lat, d)
        h = _out_proj_resid(h, y, mod[5], w_o, l)
        h = _ffn(h, mod[6], mod[7], mod[8], w2i, w2o, l)
        if not last:
            hc = _out_proj_resid(hc, yc, modc[5], w_o, l)
            hc = _ffn(hc, modc[6], modc[7], modc[8], w2i, w2o, l)
    return _final_norm(h, final_gain)
```
